```python
import jax, jax.numpy as jnp
from jax import lax
import numpy as np

D_MODEL = 1024
BATCH = 8
SEQ = 2048
DEPTH = 1
DEC_BATCH = 128
DEC_SEQ = 4
PAST_LEN = 16384
PAGE_SIZE = 128

CONV_WIDTH = D_MODEL // 2
CONV_GROUPS = 4
CONV_K = 3
DN_HEADS = 4
DN_HEAD_DIM = D_MODEL // 8
DN_WIDTH = DN_HEADS * DN_HEAD_DIM
DN_CONV_K = 4
DN_CHUNK = 64
PEER_HEADS = 8
PEER_NKEYS = 128
PEER_TOPK = 16
PEER_QDIM = 128
PEER_HALF = PEER_QDIM // 2
PEER_EXPERTS = PEER_NKEYS * PEER_NKEYS
PEER_BLOCK = 256
EPS = 1e-6

SPLIT_SIZES = (CONV_WIDTH, CONV_WIDTH, CONV_WIDTH, 3 * DN_WIDTH, DN_WIDTH, DN_HEADS, DN_HEADS, 2 * D_MODEL)
IN_WIDTH = 3 * CONV_WIDTH + 4 * DN_WIDTH + 2 * DN_HEADS + 2 * D_MODEL

kernel_name = "hybrid_conv_deltanet_peer_step"


def rmsnorm(x, g):
    xf = x.astype(jnp.float32)
    y = xf * lax.rsqrt(jnp.mean(xf * xf, axis=-1, keepdims=True) + EPS) * g.astype(jnp.float32)
    return y.astype(x.dtype)


def l2norm(x):
    return x * lax.rsqrt(jnp.sum(x * x, axis=-1, keepdims=True) + EPS)


def causal_conv(x, buf, w):
    T = x.shape[1]
    K = w.shape[0]
    xp = jnp.concatenate([buf.astype(x.dtype), x], axis=1)
    y = w[0] * xp[:, 0:T]
    for j in range(1, K):
        y = y + w[j] * xp[:, j:j + T]
    return y, xp[:, -(K - 1):]


def gated_delta_chunked(q, k, v, g, beta, s0):
    Bsz, T, H, dk = q.shape
    dv = v.shape[-1]
    C = DN_CHUNK
    n_chunks = -(-T // C)
    pad = n_chunks * C - T

    def prep(a):
        a = jnp.pad(a, [(0, 0), (0, pad)] + [(0, 0)] * (a.ndim - 2))
        a = a.reshape((Bsz, n_chunks, C) + a.shape[2:])
        return jnp.moveaxis(a, 3, 1)

    q, k, v, g, beta = prep(q), prep(k), prep(v), prep(g), prep(beta)
    G = jnp.cumsum(g, axis=-1)
    idx = jnp.arange(C)
    strict = idx[:, None] > idx[None, :]
    causal = idx[:, None] >= idx[None, :]
    diff = G[..., :, None] - G[..., None, :]
    decay = jnp.exp(jnp.where(causal, diff, -jnp.inf))
    kk = jnp.einsum('bhncd,bhnsd->bhncs', k, k)
    m = jnp.where(strict, kk * decay * beta[..., :, None], 0.0)
    a_mat = m + jnp.eye(C, dtype=m.dtype)
    rhs = jnp.concatenate([v * beta[..., None], k * (beta * jnp.exp(G))[..., None]], axis=-1)
    sol = lax.linalg.triangular_solve(a_mat, rhs, left_side=True, lower=True, unit_diagonal=True)
    u, w = sol[..., :dv], sol[..., dv:]
    qk = jnp.einsum('bhncd,bhnsd->bhncs', q, k) * decay
    q_dec = q * jnp.exp(G)[..., None]
    k_dec = k * jnp.exp(G[..., -1:] - G)[..., None]
    g_last = jnp.exp(G[..., -1])

    def step(S, xs):
        u_c, w_c, qk_c, qd_c, kd_c, gl_c = xs
        v_new = u_c - jnp.einsum('bhcd,bhde->bhce', w_c, S)
        o = jnp.einsum('bhcd,bhde->bhce', qd_c, S) + jnp.einsum('bhcs,bhse->bhce', qk_c, v_new)
        S = S * gl_c[..., None, None] + jnp.einsum('bhcd,bhce->bhde', kd_c, v_new)
        return S, o

    xs = tuple(jnp.moveaxis(a, 2, 0) for a in (u, w, qk, q_dec, k_dec, g_last))
    S, o = lax.scan(step, s0, xs)
    o = jnp.transpose(o, (1, 0, 3, 2, 4)).reshape(Bsz, n_chunks * C, H, dv)[:, :T]
    return o, S


def peer(xn, w_query, sub_keys, expert_u, expert_v):
    shape = xn.shape
    xf = xn.reshape(-1, D_MODEL)
    n = xf.shape[0]
    nb = -(-n // PEER_BLOCK)
    xf = jnp.pad(xf, ((0, nb * PEER_BLOCK - n), (0, 0))).reshape(nb, PEER_BLOCK, D_MODEL)

    def block(xb):
        q = (xb @ w_query).reshape(PEER_BLOCK, PEER_HEADS, 2, PEER_HALF)
        s = jnp.einsum('bhpd,phnd->bhpn', q, sub_keys).astype(jnp.float32)
        top_s, top_i = lax.top_k(s, PEER_TOPK)
        cand = top_s[:, :, 0, :, None] + top_s[:, :, 1, None, :]
        best_s, best_j = lax.top_k(cand.reshape(PEER_BLOCK, PEER_HEADS, PEER_TOPK * PEER_TOPK), PEER_TOPK)
        e1 = jnp.take_along_axis(top_i[:, :, 0, :], best_j // PEER_TOPK, axis=-1)
        e2 = jnp.take_along_axis(top_i[:, :, 1, :], best_j % PEER_TOPK, axis=-1)
        eid = e1 * PEER_NKEYS + e2
        gate = jax.nn.softmax(best_s, axis=-1)
        u_sel = expert_u[eid]
        act = jax.nn.gelu(jnp.einsum('bd,bhkd->bhk', xb, u_sel).astype(jnp.float32), approximate=False)
        v_sel = expert_v[eid]
        return jnp.einsum('bhk,bhkd->bd', (gate * act).astype(xb.dtype), v_sel)

    out = lax.map(block, xf)
    return out.reshape(-1, D_MODEL)[:n].reshape(shape)


def decoder_layer(x, buf_a, buf_qkv, s0, g_norm_mix, w_in, w_conv_a, w_conv_qkv, a_log, dt_bias,
                  g_dn_norm, w_out_a, w_out_dn, w_o, g_norm_ffn, w_query, sub_keys, expert_u, expert_v):
    Bsz, T, _ = x.shape
    xn = rmsnorm(x, g_norm_mix)
    proj = xn @ w_in
    offs = [int(o) for o in np.cumsum(SPLIT_SIZES)[:-1]]
    b_c, c_c, h_c, qkv, z, a_raw, b_raw, gates = jnp.split(proj, offs, axis=-1)

    conv_out, new_buf_a = causal_conv(c_c * h_c, buf_a, w_conv_a)
    y_a = (b_c * conv_out) @ w_out_a

    qkv_c, new_buf_qkv = causal_conv(qkv, buf_qkv, w_conv_qkv)
    qkv_c = jax.nn.silu(qkv_c).astype(jnp.float32)
    q, k, v = jnp.split(qkv_c, 3, axis=-1)
    q = l2norm(q.reshape(Bsz, T, DN_HEADS, DN_HEAD_DIM)) * (DN_HEAD_DIM ** -0.5)
    k = l2norm(k.reshape(Bsz, T, DN_HEADS, DN_HEAD_DIM))
    v = v.reshape(Bsz, T, DN_HEADS, DN_HEAD_DIM)
    g = -jnp.exp(a_log.astype(jnp.float32)) * jax.nn.softplus(a_raw.astype(jnp.float32) + dt_bias.astype(jnp.float32))
    beta = jax.nn.sigmoid(b_raw.astype(jnp.float32))
    o, s_new = gated_delta_chunked(q, k, v, g, beta, s0.astype(jnp.float32))
    zf = z.astype(jnp.float32).reshape(Bsz, T, DN_HEADS, DN_HEAD_DIM)
    o = o * lax.rsqrt(jnp.mean(o * o, axis=-1, keepdims=True) + EPS) * g_dn_norm.astype(jnp.float32) * jax.nn.silu(zf)
    y_b = o.reshape(Bsz, T, DN_WIDTH).astype(x.dtype) @ w_out_dn

    gate_a, gate_b = jnp.split(jax.nn.sigmoid(gates), 2, axis=-1)
    h = x + (gate_a * y_a + gate_b * y_b) @ w_o

    h = h + peer(rmsnorm(h, g_norm_ffn), w_query, sub_keys, expert_u, expert_v)
    return h, new_buf_a, new_buf_qkv, s_new.astype(x.dtype)


def trunk(x, bufs_a, bufs_qkv, states, weights, g_norm_final):
    new_a, new_qkv, new_s = [], [], []
    for l in range(DEPTH):
        x, a, c, s = decoder_layer(x, bufs_a[l], bufs_qkv[l], states[l], *[w[l] for w in weights])
        new_a.append(a)
        new_qkv.append(c)
        new_s.append(s)
    return rmsnorm(x, g_norm_final), jnp.stack(new_a), jnp.stack(new_qkv), jnp.stack(new_s)


def setup_inputs(seed: int = 0) -> dict:
    key = jax.random.key(seed)
    ks = jax.random.split(key, 24)
    f32 = jnp.float32
    nrm = lambda k, shp, s: jax.random.normal(k, shp, f32) * s
    return {
        "x_prompt": nrm(ks[0], (BATCH, SEQ, D_MODEL), 1.0),
        "x_sample": nrm(ks[1], (DEC_BATCH, DEC_SEQ, D_MODEL), 1.0),
        "state_conv_a": nrm(ks[2], (DEPTH, DEC_BATCH, CONV_K - 1, CONV_WIDTH), 1.0),
        "state_conv_qkv": nrm(ks[3], (DEPTH, DEC_BATCH, DN_CONV_K - 1, 3 * DN_WIDTH), 1.0),
        "state_delta": nrm(ks[4], (DEPTH, DEC_BATCH, DN_HEADS, DN_HEAD_DIM, DN_HEAD_DIM), 0.3),
        "g_norm_mix": 1.0 + nrm(ks[5], (DEPTH, D_MODEL), 0.02),
        "w_in": nrm(ks[6], (DEPTH, D_MODEL, IN_WIDTH), D_MODEL ** -0.5),
        "w_conv_a": nrm(ks[7], (DEPTH, CONV_K, CONV_WIDTH), CONV_K ** -0.5),
        "w_conv_qkv": nrm(ks[8], (DEPTH, DN_CONV_K, 3 * DN_WIDTH), DN_CONV_K ** -0.5),
        "a_log": jnp.log(jax.random.uniform(ks[9], (DEPTH, DN_HEADS), f32, 1.0, 16.0)),
        "dt_bias": nrm(ks[10], (DEPTH, DN_HEADS), 0.1),
        "g_dn_norm": 1.0 + nrm(ks[11], (DEPTH, DN_HEAD_DIM), 0.02),
        "w_out_a": nrm(ks[12], (DEPTH, CONV_WIDTH, D_MODEL), CONV_WIDTH ** -0.5),
        "w_out_dn": nrm(ks[13], (DEPTH, DN_WIDTH, D_MODEL), DN_WIDTH ** -0.5),
        "w_o": nrm(ks[14], (DEPTH, D_MODEL, D_MODEL), D_MODEL ** -0.5),
        "g_norm_ffn": 1.0 + nrm(ks[15], (DEPTH, D_MODEL), 0.02),
        "w_query": nrm(ks[16], (DEPTH, D_MODEL, PEER_HEADS * PEER_QDIM), D_MODEL ** -0.5),
        "sub_keys": nrm(ks[17], (DEPTH, 2, PEER_HEADS, PEER_NKEYS, PEER_HALF), PEER_HALF ** -0.5),
        "expert_u": nrm(ks[18], (DEPTH, PEER_EXPERTS, D_MODEL), D_MODEL ** -0.5),
        "expert_v": nrm(ks[19], (DEPTH, PEER_EXPERTS, D_MODEL), 0.5),
        "g_norm_final": 1.0 + nrm(ks[20], (D_MODEL,), 0.02),
    }


def reference(x_prompt, x_sample, state_conv_a, state_conv_qkv, state_delta, g_norm_mix, w_in,
              w_conv_a, w_conv_qkv, a_log, dt_bias, g_dn_norm, w_out_a, w_out_dn, w_o, g_norm_ffn,
              w_query, sub_keys, expert_u, expert_v, g_norm_final):
    weights = (g_norm_mix, w_in, w_conv_a, w_conv_qkv, a_log, dt_bias, g_dn_norm, w_out_a, w_out_dn,
               w_o, g_norm_ffn, w_query, sub_keys, expert_u, expert_v)
    bp = x_prompt.shape[0]
    dt = x_prompt.dtype
    zero_a = jnp.zeros((DEPTH, bp, CONV_K - 1, CONV_WIDTH), dt)
    zero_qkv = jnp.zeros((DEPTH, bp, DN_CONV_K - 1, 3 * DN_WIDTH), dt)
    zero_s = jnp.zeros((DEPTH, bp, DN_HEADS, DN_HEAD_DIM, DN_HEAD_DIM), dt)
    y_prompt, conv_a_p, conv_qkv_p, delta_p = trunk(x_prompt, zero_a, zero_qkv, zero_s, weights, g_norm_final)
    y_sample, conv_a_s, conv_qkv_s, delta_s = trunk(x_sample, state_conv_a, state_conv_qkv, state_delta,
                                                    weights, g_norm_final)
    return (y_prompt, y_sample, conv_a_p, conv_qkv_p, delta_p, conv_a_s, conv_qkv_s, delta_s)
```

```python
import functools

import jax
import jax.numpy as jnp
from jax import lax
from jax.experimental import pallas as pl
from jax.experimental.pallas import tpu as pltpu

EPS = 1e-6
F32 = jnp.float32
BF16 = jnp.bfloat16
LANES = 128
SUBLANES = 8
VMEM_LIMIT = 56 * 1024 * 1024

PEER_TOPK = 16
DN_CHUNK = 64
NEG_INF = float("-inf")


def _cparams(sem):
    return pltpu.CompilerParams(dimension_semantics=sem, vmem_limit_bytes=VMEM_LIMIT)


def _mm(a, b):
    return jnp.dot(a.astype(BF16), b.astype(BF16), preferred_element_type=F32)


def _mm_nt(a, b):
    return lax.dot_general(a.astype(BF16), b.astype(BF16), (((1,), (1,)), ((), ())), preferred_element_type=F32)


def _mm_tn(a, b):
    return lax.dot_general(a.astype(BF16), b.astype(BF16), (((0,), (0,)), ((), ())), preferred_element_type=F32)


def _rmsnorm(x, g):
    return x * lax.rsqrt(jnp.mean(x * x, axis=-1, keepdims=True) + EPS) * g


def _sigmoid(x):
    return 1.0 / (1.0 + jnp.exp(-x))


def _silu(x):
    return x * _sigmoid(x)


def _softplus(x):
    return jnp.maximum(x, 0.0) + jnp.log(1.0 + jnp.exp(-jnp.abs(x)))


def _gelu_exact(x):
    return 0.5 * x * (1.0 + lax.erf(x * (2.0 ** -0.5)))


def _in_proj_kernel(x_ref, g_ref, w_ref, o_ref, xn_ref):
    @pl.when(pl.program_id(1) == 0)
    def _():
        xn_ref[...] = _rmsnorm(x_ref[...], g_ref[...]).astype(BF16)

    o_ref[...] = jnp.dot(xn_ref[...], w_ref[...], preferred_element_type=F32)


def _in_proj(x2d, g, w, tm, tn):
    n, d = x2d.shape
    width = w.shape[1]
    return pl.pallas_call(
        _in_proj_kernel,
        grid=(n // tm, width // tn),
        in_specs=[
            pl.BlockSpec((tm, d), lambda i, j: (i, 0)),
            pl.BlockSpec((1, d), lambda i, j: (0, 0)),
            pl.BlockSpec((d, tn), lambda i, j: (0, j)),
        ],
        out_specs=pl.BlockSpec((tm, tn), lambda i, j: (i, j)),
        out_shape=jax.ShapeDtypeStruct((n, width), F32),
        scratch_shapes=[pltpu.VMEM((tm, d), BF16)],
        compiler_params=_cparams(("parallel", "arbitrary")),
        name="in_proj",
    )(x2d, g, w)


def _shift_rows(xp, k, rows):
    if k == 0:
        return xp[SUBLANES:SUBLANES + rows]
    return pltpu.roll(xp, k, axis=0)[SUBLANES:SUBLANES + rows]


def _cumsum_rows(x, rows):
    ridx = lax.broadcasted_iota(jnp.int32, x.shape, 0)
    s = 1
    while s < rows:
        x = x + jnp.where(ridx >= s, pltpu.roll(x, s, axis=0), 0.0)
        s *= 2
    return x


def _unit_lower_inverse(m, c):
    eye = (lax.broadcasted_iota(jnp.int32, (c, c), 0) == lax.broadcasted_iota(jnp.int32, (c, c), 1)).astype(F32)
    p = -m
    t = eye + p
    s = 2
    while s < c:
        p = _mm(p, p)
        t = t + _mm(t, p)
        s *= 2
    return t


def _mixer_kernel(bch_ref, qkv_ref, z_ref, ab_ref, bufa_ref, bufq_ref, s0_ref,
                  wca_ref, wcq_ref, alog_ref, dtb_ref, gdn_ref,
                  ya_ref, og_ref, nbufa_ref, nbufq_ref, sout_ref,
                  halo_a, halo_q, state,
                  *, bb, c, heads, hd, t_real, n_chunks):
    ci = pl.program_id(1)
    cw = heads * hd
    ka = wca_ref.shape[0]
    kq = wcq_ref.shape[0]

    @pl.when(ci == 0)
    def _():
        state[...] = s0_ref[...]
        halo_a[...] = jnp.zeros(halo_a.shape, F32)
        halo_q[...] = jnp.zeros(halo_q.shape, F32)
        halo_a[:, SUBLANES - (ka - 1):, :] = bufa_ref[...]
        halo_q[:, SUBLANES - (kq - 1):, :] = bufq_ref[...]

    wca = wca_ref[...]
    wcq = wcq_ref[...]
    neg_a = -jnp.exp(alog_ref[...])
    dtb = dtb_ref[...]
    gdn = gdn_ref[...]
    rows_left = t_real - ci * c
    ridx = lax.broadcasted_iota(jnp.int32, (c, 1), 0)
    valid = ridx < rows_left
    ri = lax.broadcasted_iota(jnp.int32, (c, c), 0)
    si = lax.broadcasted_iota(jnp.int32, (c, c), 1)
    causal = ri >= si
    strict = ri > si
    cpad = max(c, LANES)

    def per_seq(b, carry):
        bch = bch_ref[b]
        b_c, c_c, h_c = bch[:, :cw], bch[:, cw:2 * cw], bch[:, 2 * cw:]
        ta = c_c * h_c
        xpa = jnp.concatenate([halo_a[b], ta], axis=0)
        conv_a = wca[ka - 1:ka] * _shift_rows(xpa, 0, c)
        for j in range(1, ka):
            conv_a = conv_a + wca[ka - 1 - j:ka - j] * _shift_rows(xpa, j, c)
        ya_ref[b] = b_c * conv_a
        qkv = qkv_ref[b]
        xpq = jnp.concatenate([halo_q[b], qkv], axis=0)
        conv_q = wcq[kq - 1:kq] * _shift_rows(xpq, 0, c)
        for j in range(1, kq):
            conv_q = conv_q + wcq[kq - 1 - j:kq - j] * _shift_rows(xpq, j, c)
        conv_q = _silu(conv_q)
        if c >= SUBLANES:
            halo_a[b] = xpa[c:c + SUBLANES]
            halo_q[b] = xpq[c:c + SUBLANES]
        ab = ab_ref[b]
        g_all = jnp.where(valid, neg_a * _softplus(ab + dtb), 0.0)
        beta_all = jnp.where(valid, _sigmoid(ab), 0.0)
        gcum = _cumsum_rows(g_all, c)
        if c < cpad:
            gpad = jnp.concatenate([gcum, jnp.zeros((cpad - c, LANES), F32)], axis=0)
        else:
            gpad = gcum
        gcum_t = gpad.T
        z = z_ref[b]
        for h in range(heads):
            sl = slice(h * hd, (h + 1) * hd)
            q = conv_q[:, h * hd:(h + 1) * hd]
            k = conv_q[:, cw + h * hd:cw + (h + 1) * hd]
            v = conv_q[:, 2 * cw + h * hd:2 * cw + (h + 1) * hd]
            q = q * lax.rsqrt(jnp.sum(q * q, axis=-1, keepdims=True) + EPS) * (hd ** -0.5)
            k = k * lax.rsqrt(jnp.sum(k * k, axis=-1, keepdims=True) + EPS)
            q = jnp.where(valid, q, 0.0)
            k = jnp.where(valid, k, 0.0)
            v = jnp.where(valid, v, 0.0)
            gc = gcum[:, h:h + 1]
            gr = gcum_t[h:h + 1, :c]
            beta = beta_all[:, heads + h:heads + h + 1]
            g_last = gcum[c - 1:c, h:h + 1]
            decay = jnp.exp(jnp.where(causal, gc - gr, NEG_INF))
            kk = _mm_nt(k, k)
            m = jnp.where(strict, kk * decay * beta, 0.0)
            t_inv = _unit_lower_inverse(m, c)
            eg = jnp.exp(gc)
            rhs = jnp.concatenate([v * beta, k * (beta * eg)], axis=-1)
            sol = _mm(t_inv, rhs)
            u, w = sol[:, :hd], sol[:, hd:]
            qk = _mm_nt(q, k) * decay
            q_dec = q * eg
            k_dec = k * jnp.exp(g_last - gc)
            s_prev = state[b, h]
            v_new = u - _mm(w, s_prev)
            o = _mm(q_dec, s_prev) + _mm(qk, v_new)
            state[b, h] = s_prev * jnp.exp(g_last) + _mm_tn(k_dec, v_new)
            o = o * lax.rsqrt(jnp.mean(o * o, axis=-1, keepdims=True) + EPS) * gdn * _silu(z[:, sl])
            og_ref[b, :, sl] = o
        return carry

    lax.fori_loop(0, bb, per_seq, 0)

    @pl.when(ci == n_chunks - 1)
    def _():
        sout_ref[...] = state[...]
        last = t_real - (n_chunks - 1) * c
        nbufa_ref[...] = (bch_ref[:, last - (ka - 1):last, cw:2 * cw]
                          * bch_ref[:, last - (ka - 1):last, 2 * cw:])
        nbufq_ref[...] = qkv_ref[:, last - (kq - 1):last, :]


def _mixer(proj3, buf_a, buf_q, s0, w_conv_a, w_conv_q, a_log_pad, dt_pad, g_dn, *, bb, c, t_real, heads, hd):
    bsz, tp, _ = proj3.shape
    cw = heads * hd
    n_chunks = tp // c
    ka, kq = w_conv_a.shape[0], w_conv_q.shape[0]
    kern = functools.partial(_mixer_kernel, bb=bb, c=c, heads=heads, hd=hd, t_real=t_real, n_chunks=n_chunks)
    return pl.pallas_call(
        kern,
        grid=(bsz // bb, n_chunks),
        in_specs=[
            pl.BlockSpec((bb, c, 3 * cw), lambda i, j: (i, j, 0)),
            pl.BlockSpec((bb, c, 3 * cw), lambda i, j: (i, j, 1)),
            pl.BlockSpec((bb, c, cw), lambda i, j: (i, j, (3 * cw + 3 * cw + 4 * cw) // cw)),
            pl.BlockSpec((bb, c, LANES), lambda i, j: (i, j, (3 * cw + 3 * cw + 4 * cw + cw) // LANES)),
            pl.BlockSpec((bb, ka - 1, cw), lambda i, j: (i, 0, 0)),
            pl.BlockSpec((bb, kq - 1, 3 * cw), lambda i, j: (i, 0, 0)),
            pl.BlockSpec((bb, heads, hd, hd), lambda i, j: (i, 0, 0, 0)),
            pl.BlockSpec((ka, cw), lambda i, j: (0, 0)),
            pl.BlockSpec((kq, 3 * cw), lambda i, j: (0, 0)),
            pl.BlockSpec((1, LANES), lambda i, j: (0, 0)),
            pl.BlockSpec((1, LANES), lambda i, j: (0, 0)),
            pl.BlockSpec((1, hd), lambda i, j: (0, 0)),
        ],
        out_specs=[
            pl.BlockSpec((bb, c, cw), lambda i, j: (i, j, 0)),
            pl.BlockSpec((bb, c, cw), lambda i, j: (i, j, 0)),
            pl.BlockSpec((bb, ka - 1, cw), lambda i, j: (i, 0, 0)),
            pl.BlockSpec((bb, kq - 1, 3 * cw), lambda i, j: (i, 0, 0)),
            pl.BlockSpec((bb, heads, hd, hd), lambda i, j: (i, 0, 0, 0)),
        ],
        out_shape=[
            jax.ShapeDtypeStruct((bsz, tp, cw), F32),
            jax.ShapeDtypeStruct((bsz, tp, cw), F32),
            jax.ShapeDtypeStruct((bsz, ka - 1, cw), F32),
            jax.ShapeDtypeStruct((bsz, kq - 1, 3 * cw), F32),
            jax.ShapeDtypeStruct((bsz, heads, hd, hd), F32),
        ],
        scratch_shapes=[
            pltpu.VMEM((bb, SUBLANES, cw), F32),
            pltpu.VMEM((bb, SUBLANES, 3 * cw), F32),
            pltpu.VMEM((bb, heads, hd, hd), F32),
        ],
        compiler_params=_cparams(("parallel", "arbitrary")),
        name="mixer",
    )(proj3, proj3, proj3, proj3, buf_a, buf_q, s0, w_conv_a, w_conv_q, a_log_pad, dt_pad, g_dn)


def _out_proj_kernel(x_ref, ya_ref, og_ref, ga_ref, gb_ref, woa_ref, wod_ref, wo_ref, h_ref):
    y_a = jnp.dot(ya_ref[...].astype(BF16), woa_ref[...], preferred_element_type=F32)
    y_b = jnp.dot(og_ref[...].astype(BF16), wod_ref[...], preferred_element_type=F32)
    m = _sigmoid(ga_ref[...]) * y_a + _sigmoid(gb_ref[...]) * y_b
    h_ref[...] = x_ref[...] + jnp.dot(m.astype(BF16), wo_ref[...], preferred_element_type=F32)


def _out_proj(x2d, ya, og, proj, w_out_a, w_out_dn, w_o, tm, gate_block):
    n, d = x2d.shape
    cw = ya.shape[1]
    return pl.pallas_call(
        _out_proj_kernel,
        grid=(n // tm,),
        in_specs=[
            pl.BlockSpec((tm, d), lambda i: (i, 0)),
            pl.BlockSpec((tm, cw), lambda i: (i, 0)),
            pl.BlockSpec((tm, cw), lambda i: (i, 0)),
            pl.BlockSpec((tm, d), lambda i: (i, gate_block)),
            pl.BlockSpec((tm, d), lambda i: (i, gate_block + 1)),
            pl.BlockSpec((cw, d), lambda i: (0, 0)),
            pl.BlockSpec((cw, d), lambda i: (0, 0)),
            pl.BlockSpec((d, d), lambda i: (0, 0)),
        ],
        out_specs=pl.BlockSpec((tm, d), lambda i: (i, 0)),
        out_shape=jax.ShapeDtypeStruct((n, d), F32),
        compiler_params=_cparams(("parallel",)),
        name="out_proj",
    )(x2d, ya, og, proj, proj, w_out_a, w_out_dn, w_o)


def _topk_rows(s, topk):
    n, t = s.shape
    io = lax.broadcasted_iota(jnp.int32, (n, t), 0)
    ro = lax.broadcasted_iota(jnp.int32, (topk, t), 0)
    rank = jnp.full((n, t), float(topk), F32)
    vals = jnp.zeros((topk, t), F32)
    for r in range(topk):
        m = jnp.max(s, axis=0, keepdims=True)
        idx = jnp.min(jnp.where(s == m, io, n), axis=0, keepdims=True)
        hit = io == idx
        rank = jnp.where(hit, float(r), rank)
        s = jnp.where(hit, NEG_INF, s)
        vals = jnp.where(ro == r, m, vals)
    return vals, rank


def _pair_select(v1, v2, topk):
    t = v1.shape[1]
    io8 = lax.broadcasted_iota(jnp.int32, (SUBLANES, 1), 0)
    pieces, flats, valids = [], [], []
    pieces.append(v1[0:1] + v2)
    flats.append(lax.broadcasted_iota(jnp.int32, (topk, 1), 0))
    valids.append(jnp.ones((topk, 1), jnp.bool_))
    for a in range(1, SUBLANES):
        pieces.append(v1[a:a + 1] + v2[0:SUBLANES])
        flats.append(a * topk + io8)
        valids.append(io8 < (topk // (a + 1)))
    pieces.append(v1[SUBLANES:] + v2[0:1])
    flats.append((SUBLANES + io8) * topk)
    valids.append(jnp.ones((SUBLANES, 1), jnp.bool_))
    sums = jnp.concatenate(pieces, axis=0)
    flat = jnp.concatenate(flats, axis=0)
    valid = jnp.concatenate(valids, axis=0)
    big = topk * topk
    cand = jnp.where(valid, sums, NEG_INF)
    sel = jnp.zeros(cand.shape, F32)
    for _ in range(topk):
        m = jnp.max(cand, axis=0, keepdims=True)
        idx = jnp.min(jnp.where(cand == m, flat, big), axis=0, keepdims=True)
        hit = flat == idx
        sel = jnp.where(hit, 1.0, sel)
        cand = jnp.where(hit, NEG_INF, cand)
    mtot = v1[0:1] + v2[0:1]
    z = jnp.sum(sel * jnp.exp(jnp.where(valid, sums, 0.0) - mtot), axis=0, keepdims=True)
    ro = lax.broadcasted_iota(jnp.int32, (topk, t), 0)
    nb = jnp.zeros((topk, t), F32)
    nb = jnp.where(ro == 0, jnp.sum(sel[0:topk], axis=0, keepdims=True), nb)
    for a in range(1, SUBLANES):
        lo = topk + (a - 1) * SUBLANES
        nb = jnp.where(ro == a, jnp.sum(sel[lo:lo + SUBLANES], axis=0, keepdims=True), nb)
    tail = sel[topk + (SUBLANES - 1) * SUBLANES:]
    nb = jnp.where(ro >= SUBLANES, jnp.concatenate([tail, tail], axis=0), nb)
    return nb, z


def _peer_select_kernel(h_ref, g_ref, wqt_ref, keys_ref, xn_ref, r2_ref, e2_ref, cnt_ref, c1_ref, q_scr,
                        *, heads, half, topk):
    xn = _rmsnorm(h_ref[...], g_ref[...]).astype(BF16)
    xn_ref[...] = xn
    q_scr[...] = lax.dot_general(wqt_ref[...], xn, (((1,), (1,)), ((), ())), preferred_element_type=F32)

    def per_head(h, carry):
        base = pl.multiple_of(h * 2 * half, 2 * half)
        q1 = q_scr[pl.ds(base, half), :]
        q2 = q_scr[pl.ds(base + half, half), :]
        s1 = _mm(keys_ref[h], q1)
        s2 = _mm(keys_ref[heads + h], q2)
        v1, r1 = _topk_rows(s1, topk)
        v2, r2 = _topk_rows(s2, topk)
        nb, z = _pair_select(v1, v2, topk)
        cnt = jnp.zeros(s1.shape, F32)
        for a in range(topk):
            cnt = jnp.where(r1 == float(a), nb[a:a + 1], cnt)
        r2_ref[h] = r2
        e2_ref[h] = jnp.exp(s2 - v2[0:1])
        cnt_ref[h] = cnt
        c1_ref[h] = jnp.exp(s1 - v1[0:1]) / z
        return carry

    lax.fori_loop(0, heads, per_head, 0)


def _peer_select(h2d, g, wq_t, keys, *, heads, nkeys, half, topk, tms):
    n, d = h2d.shape
    qd = wq_t.shape[0]
    kern = functools.partial(_peer_select_kernel, heads=heads, half=half, topk=topk)
    sel_shape = jax.ShapeDtypeStruct((heads, nkeys, n), F32)
    sel_spec = pl.BlockSpec((heads, nkeys, tms), lambda i: (0, 0, i))
    return pl.pallas_call(
        kern,
        grid=(n // tms,),
        in_specs=[
            pl.BlockSpec((tms, d), lambda i: (i, 0)),
            pl.BlockSpec((1, d), lambda i: (0, 0)),
            pl.BlockSpec((qd, d), lambda i: (0, 0)),
            pl.BlockSpec(keys.shape, lambda i: (0, 0, 0)),
        ],
        out_specs=[pl.BlockSpec((tms, d), lambda i: (i, 0)), sel_spec, sel_spec, sel_spec, sel_spec],
        out_shape=[jax.ShapeDtypeStruct((n, d), BF16), sel_shape, sel_shape, sel_shape, sel_shape],
        scratch_shapes=[pltpu.VMEM((qd, tms), F32)],
        compiler_params=_cparams(("parallel",)),
        name="peer_select",
    )(h2d, g, wq_t, keys)


def _peer_dense_kernel(xn_ref, u_ref, vt_ref, r2_ref, e2_ref, cnt_ref, c1_ref, h_ref, g_ref, y_ref,
                       acc_ref, ht_ref, w_ref, *, heads, nkeys, eb, tm):
    e = pl.program_id(1)

    @pl.when(e == 0)
    def _():
        acc_ref[...] = jnp.zeros(acc_ref.shape, F32)

    ht_ref[...] = lax.dot_general(u_ref[...], xn_ref[...], (((1,), (1,)), ((), ())), preferred_element_type=F32)
    i0 = pl.multiple_of(e * (eb // nkeys), SUBLANES)
    for lc in range(tm // LANES):
        ls = slice(lc * LANES, (lc + 1) * LANES)
        for ii in range(eb // nkeys):
            rs = slice(ii * nkeys, (ii + 1) * nkeys)
            gate = jnp.zeros((nkeys, LANES), F32)
            for h in range(heads):
                cnt = cnt_ref[h, pl.ds(i0, SUBLANES), ls][ii:ii + 1]
                c1 = c1_ref[h, pl.ds(i0, SUBLANES), ls][ii:ii + 1]
                gate = gate + jnp.where(r2_ref[h, :, ls] < cnt, e2_ref[h, :, ls], 0.0) * c1
            w_ref[rs, ls] = (gate * _gelu_exact(ht_ref[rs, ls])).astype(BF16)
    acc_ref[...] += jnp.dot(vt_ref[...], w_ref[...], preferred_element_type=F32)

    @pl.when(e == pl.num_programs(1) - 1)
    def _():
        y_ref[...] = _rmsnorm(h_ref[...] + acc_ref[...].T, g_ref[...])


def _peer_dense(xn, u, vt, r2, e2, cnt, c1, h2d, g, *, heads, nkeys, eb, tm):
    n, d = h2d.shape
    n_exp = u.shape[0]
    assert eb == SUBLANES * nkeys
    kern = functools.partial(_peer_dense_kernel, heads=heads, nkeys=nkeys, eb=eb, tm=tm)
    sel_spec = pl.BlockSpec((heads, nkeys, tm), lambda i, j: (0, 0, i))
    return pl.pallas_call(
        kern,
        grid=(n // tm, n_exp // eb),
        in_specs=[
            pl.BlockSpec((tm, d), lambda i, j: (i, 0)),
            pl.BlockSpec((eb, d), lambda i, j: (j, 0)),
            pl.BlockSpec((d, eb), lambda i, j: (0, j)),
            sel_spec, sel_spec, sel_spec, sel_spec,
            pl.BlockSpec((tm, d), lambda i, j: (i, 0)),
            pl.BlockSpec((1, d), lambda i, j: (0, 0)),
        ],
        out_specs=pl.BlockSpec((tm, d), lambda i, j: (i, 0)),
        out_shape=jax.ShapeDtypeStruct((n, d), F32),
        scratch_shapes=[
            pltpu.VMEM((d, tm), F32),
            pltpu.VMEM((eb, tm), F32),
            pltpu.VMEM((eb, tm), BF16),
        ],
        compiler_params=_cparams(("parallel", "arbitrary")),
        name="peer_dense",
    )(xn, u, vt, r2, e2, cnt, c1, h2d, g)


def _pick_block(n, candidates):
    for c in candidates:
        if n % c == 0:
            return c
    raise ValueError(f"no block size in {candidates} divides {n}")


def _trunk(x, buf_a, buf_q, s0, t_real, wts, dims):
    heads, hd, cw, p_heads, nkeys, half = dims
    (g_mix, w_in, w_conv_a, w_conv_q, a_log_pad, dt_pad, g_dn, w_out_a, w_out_dn, w_o, g_ffn,
     wq_t, keys, u_bf, vt_bf, g_final) = wts
    bsz, tp, d = x.shape
    n = bsz * tp
    x2d = x.reshape(n, d)
    c = DN_CHUNK if tp % DN_CHUNK == 0 else tp
    ka, kq = w_conv_a.shape[0], w_conv_q.shape[0]
    last = t_real - (tp // c - 1) * c
    assert 0 < last <= c and last >= kq - 1 and last >= ka - 1, (t_real, tp, c)

    tm = _pick_block(n, (768, 512, 256, 128))
    proj = _in_proj(x2d, g_mix, w_in, tm, w_in.shape[1] // 5)
    bb = _pick_block(bsz, (8, 4, 2, 1)) if c == DN_CHUNK else _pick_block(bsz, (16, 8, 4, 2, 1))
    ya, og, nbuf_a, nbuf_q, s_new = _mixer(
        proj.reshape(bsz, tp, -1), buf_a, buf_q, s0, w_conv_a, w_conv_q, a_log_pad, dt_pad, g_dn,
        bb=bb, c=c, t_real=t_real, heads=heads, hd=hd)
    tm2 = _pick_block(n, (512, 256, 128))
    hres = _out_proj(x2d, ya.reshape(n, cw), og.reshape(n, cw), proj, w_out_a, w_out_dn, w_o, tm2,
                     gate_block=(6 * cw) // d)
    xn, r2, e2, cnt, c1 = _peer_select(hres, g_ffn, wq_t, keys, heads=p_heads, nkeys=nkeys, half=half,
                                       topk=PEER_TOPK, tms=LANES)
    tm3 = _pick_block(n, (512, 256, 128))
    eb = _pick_block(u_bf.shape[0], (1024, 512, 256, 128))
    y = _peer_dense(xn, u_bf, vt_bf, r2, e2, cnt, c1, hres, g_final, heads=p_heads, nkeys=nkeys, eb=eb, tm=tm3)
    return y.reshape(bsz, tp, d), nbuf_a, nbuf_q, s_new


def kernel(x_prompt, x_sample, state_conv_a, state_conv_qkv, state_delta, g_norm_mix, w_in, w_conv_a, w_conv_qkv,
           a_log, dt_bias, g_dn_norm, w_out_a, w_out_dn, w_o, g_norm_ffn, w_query, sub_keys, expert_u, expert_v,
           g_norm_final):
    depth = w_in.shape[0]
    assert depth == 1
    d = x_prompt.shape[-1]
    heads = a_log.shape[1]
    hd = g_dn_norm.shape[1]
    cw = heads * hd
    assert w_conv_a.shape[2] == cw and w_conv_qkv.shape[2] == 3 * cw and d == 2 * cw
    p_heads, nkeys, half = sub_keys.shape[2], sub_keys.shape[3], sub_keys.shape[4]
    assert nkeys == LANES and expert_u.shape[1] == nkeys * nkeys

    w = w_in[0]
    o_qkv, o_z, o_a, o_b, o_g = 3 * cw, 6 * cw, 7 * cw, 7 * cw + heads, 7 * cw + 2 * heads
    ab_pad = jnp.zeros((d, LANES - 2 * heads), w.dtype)
    w_r = jnp.concatenate([w[:, :o_z], w[:, o_g:], w[:, o_z:o_a], w[:, o_a:o_g], ab_pad], axis=1).astype(BF16)
    pad4 = lambda v: jnp.pad(v.astype(F32), ((0, 0), (0, LANES - v.shape[1])))
    wts = (
        g_norm_mix, w_r, w_conv_a[0], w_conv_qkv[0], pad4(a_log), pad4(dt_bias), g_dn_norm,
        w_out_a[0].astype(BF16), w_out_dn[0].astype(BF16), w_o[0].astype(BF16), g_norm_ffn,
        w_query[0].T.astype(BF16), sub_keys[0].reshape(2 * p_heads, nkeys, half).astype(BF16),
        expert_u[0].astype(BF16), expert_v[0].T.astype(BF16), g_norm_final.reshape(1, d),
    )
    dims = (heads, hd, cw, p_heads, nkeys, half)

    bp, tpr, _ = x_prompt.shape
    assert tpr % DN_CHUNK == 0
    ka, kq = w_conv_a.shape[1], w_conv_qkv.shape[1]
    zero_a = jnp.zeros((bp, ka - 1, cw), F32)
    zero_q = jnp.zeros((bp, kq - 1, 3 * cw), F32)
    zero_s = jnp.zeros((bp, heads, hd, hd), F32)
    y_p, a_p, q_p, s_p = _trunk(x_prompt, zero_a, zero_q, zero_s, tpr, wts, dims)

    bs, ts, _ = x_sample.shape
    assert ts <= SUBLANES
    xs = jnp.pad(x_sample, ((0, 0), (0, SUBLANES - ts), (0, 0)))
    y_s, a_s, q_s, s_s = _trunk(xs, state_conv_a[0], state_conv_qkv[0], state_delta[0], ts, wts, dims)
    y_s = y_s[:, :ts]
    return (y_p, y_s, a_p[None], q_p[None], s_p[None], a_s[None], q_s[None], s_s[None])
```

```python
import functools

import jax
import jax.numpy as jnp
from jax import lax
from jax.experimental import pallas as pl
from jax.experimental.pallas import tpu as pltpu

EPS = 1e-6
F32 = jnp.float32
BF16 = jnp.bfloat16
LANES = 128
SUBLANES = 8
VMEM_LIMIT = 56 * 1024 * 1024

PEER_TOPK = 16
DN_CHUNK = 128
MM_PIECES = 4
NEG_INF = float("-inf")


def _cparams(sem):
    return pltpu.CompilerParams(dimension_semantics=sem, vmem_limit_bytes=VMEM_LIMIT)


def _mm(a, b):
    return jnp.dot(a.astype(BF16), b.astype(BF16), preferred_element_type=F32)


def _mm_nt(a, b):
    return lax.dot_general(a.astype(BF16), b.astype(BF16), (((1,), (1,)), ((), ())), preferred_element_type=F32)


def _mm_tn(a, b):
    return lax.dot_general(a.astype(BF16), b.astype(BF16), (((0,), (0,)), ((), ())), preferred_element_type=F32)


def _pack_rows(x):
    m2, n = x.shape
    return lax.bitcast_convert_type(jnp.swapaxes(x.reshape(m2 // 2, 2, n), -1, -2), jnp.uint32)


def _rmsnorm(x, g):
    return x * lax.rsqrt(jnp.mean(x * x, axis=-1, keepdims=True) + EPS) * g


def _sigmoid(x):
    return 1.0 / (1.0 + jnp.exp(-x))


def _silu(x):
    return x * _sigmoid(x)


def _softplus(x):
    return jnp.maximum(x, 0.0) + jnp.log(1.0 + jnp.exp(-jnp.abs(x)))


def _gelu_exact(x):
    return 0.5 * x * (1.0 + lax.erf(x * (2.0 ** -0.5)))


def _in_proj_kernel(x_ref, g_ref, w_ref, o_ref, xn_ref):
    @pl.when(pl.program_id(1) == 0)
    def _():
        xn_ref[...] = _rmsnorm(x_ref[...], g_ref[...]).astype(BF16)

    o_ref[...] = jnp.dot(xn_ref[...], w_ref[...], preferred_element_type=F32)


def _in_proj(x2d, g, w, tm, tn):
    n, d = x2d.shape
    width = w.shape[1]
    return pl.pallas_call(
        _in_proj_kernel,
        grid=(n // tm, width // tn),
        in_specs=[
            pl.BlockSpec((tm, d), lambda i, j: (i, 0)),
            pl.BlockSpec((1, d), lambda i, j: (0, 0)),
            pl.BlockSpec((d, tn), lambda i, j: (0, j)),
        ],
        out_specs=pl.BlockSpec((tm, tn), lambda i, j: (i, j)),
        out_shape=jax.ShapeDtypeStruct((n, width), F32),
        scratch_shapes=[pltpu.VMEM((tm, d), BF16)],
        compiler_params=_cparams(("parallel", "arbitrary")),
        name="in_proj",
    )(x2d, g, w)


def _shift_rows(xp, k, rows):
    if k == 0:
        return xp[SUBLANES:SUBLANES + rows]
    return pltpu.roll(xp, k, axis=0)[SUBLANES:SUBLANES + rows]


def _cumsum_rows(x, rows):
    ridx = lax.broadcasted_iota(jnp.int32, x.shape, 0)
    s = 1
    while s < rows:
        x = x + jnp.where(ridx >= s, pltpu.roll(x, s, axis=0), 0.0)
        s *= 2
    return x


def _unit_lower_inverse_many(ms, c):
    eye = (lax.broadcasted_iota(jnp.int32, (c, c), 0) == lax.broadcasted_iota(jnp.int32, (c, c), 1)).astype(F32)
    n_factors = c.bit_length() - 1
    assert 1 << n_factors == c
    ps = [-m for m in ms]
    ts = [eye + p for p in ps]
    if n_factors == 1:
        return ts
    ps = [_mm(p, p) for p in ps]
    for _ in range(n_factors - 2):
        tps = [_mm(jnp.concatenate([t, p], axis=0), p) for t, p in zip(ts, ps)]
        ts = [t + tp[:c] for t, tp in zip(ts, tps)]
        ps = [tp[c:] for tp in tps]
    return [t + _mm(t, p) for t, p in zip(ts, ps)]


def _mixer_kernel(bch_ref, qkv_ref, z_ref, ab_ref, bufa_ref, bufq_ref, s0_ref,
                  wca_ref, wcq_ref, alog_ref, dtb_ref, gdn_ref,
                  ya_ref, og_ref, nbufa_ref, nbufq_ref, sout_ref,
                  halo_a, halo_q, state,
                  *, bb, c, heads, hd, t_real, n_chunks, unroll):
    ci = pl.program_id(1)
    cw = heads * hd
    ka = wca_ref.shape[0]
    kq = wcq_ref.shape[0]

    @pl.when(ci == 0)
    def _():
        state[...] = s0_ref[...]
        halo_a[...] = jnp.zeros(halo_a.shape, F32)
        halo_q[...] = jnp.zeros(halo_q.shape, F32)
        halo_a[:, SUBLANES - (ka - 1):, :] = bufa_ref[...]
        halo_q[:, SUBLANES - (kq - 1):, :] = bufq_ref[...]

    wca = wca_ref[...]
    wcq = wcq_ref[...]
    neg_a = -jnp.exp(alog_ref[...])
    dtb = dtb_ref[...]
    gdn = gdn_ref[...]
    rows_left = t_real - ci * c
    ridx = lax.broadcasted_iota(jnp.int32, (c, 1), 0)
    valid = ridx < rows_left
    ri = lax.broadcasted_iota(jnp.int32, (c, c), 0)
    si = lax.broadcasted_iota(jnp.int32, (c, c), 1)
    causal = ri >= si
    strict = ri > si
    cpad = max(c, LANES)

    def load_seq(b):
        return (bch_ref[b], qkv_ref[b], z_ref[b], ab_ref[b], halo_a[b], halo_q[b],
                [state[b, h] for h in range(heads)])

    def prep_seq(bch, qkv, z, ab, ha, hq, s_prev_all):
        b_c, c_c, h_c = bch[:, :cw], bch[:, cw:2 * cw], bch[:, 2 * cw:]
        ta = c_c * h_c
        xpa = jnp.concatenate([ha, ta], axis=0)
        conv_a = wca[ka - 1:ka] * _shift_rows(xpa, 0, c)
        for j in range(1, ka):
            conv_a = conv_a + wca[ka - 1 - j:ka - j] * _shift_rows(xpa, j, c)
        ya = b_c * conv_a
        xpq = jnp.concatenate([hq, qkv], axis=0)
        conv_q = wcq[kq - 1:kq] * _shift_rows(xpq, 0, c)
        for j in range(1, kq):
            conv_q = conv_q + wcq[kq - 1 - j:kq - j] * _shift_rows(xpq, j, c)
        conv_q = _silu(conv_q)
        new_ha, new_hq = xpa[c:c + SUBLANES], xpq[c:c + SUBLANES]
        g_all = jnp.where(valid, neg_a * _softplus(ab + dtb), 0.0)
        beta_all = jnp.where(valid, _sigmoid(ab), 0.0)
        gcum = _cumsum_rows(g_all, c)
        if c < cpad:
            gpad = jnp.concatenate([gcum, jnp.zeros((cpad - c, LANES), F32)], axis=0)
        else:
            gpad = gcum
        gcum_t = gpad.T
        chains = []
        for h in range(heads):
            q = conv_q[:, h * hd:(h + 1) * hd]
            k = conv_q[:, cw + h * hd:cw + (h + 1) * hd]
            v = conv_q[:, 2 * cw + h * hd:2 * cw + (h + 1) * hd]
            q = q * lax.rsqrt(jnp.sum(q * q, axis=-1, keepdims=True) + EPS) * (hd ** -0.5)
            k = k * lax.rsqrt(jnp.sum(k * k, axis=-1, keepdims=True) + EPS)
            q = jnp.where(valid, q, 0.0)
            k = jnp.where(valid, k, 0.0)
            v = jnp.where(valid, v, 0.0)
            gc = gcum[:, h:h + 1]
            gr = gcum_t[h:h + 1, :c]
            beta = beta_all[:, heads + h:heads + h + 1]
            g_last = gcum[c - 1:c, h:h + 1]
            eg = jnp.exp(gc)
            chains.append(dict(
                q=q, k=k, beta=beta, g_last=g_last, s_prev=s_prev_all[h], z=z[:, h * hd:(h + 1) * hd],
                decay=jnp.exp(jnp.where(causal, gc - gr, NEG_INF)),
                rhs=jnp.concatenate([v * beta, k * (beta * eg)], axis=-1),
                q_dec=q * eg, k_dec=k * jnp.exp(g_last - gc)))
        return ya, new_ha, new_hq, chains

    def delta_chains(ch):
        kkqk = [_mm_nt(jnp.concatenate([x["k"], x["q"]], axis=0), x["k"]) for x in ch]
        ms = [jnp.where(strict, kq_[:c] * x["decay"] * x["beta"], 0.0) for kq_, x in zip(kkqk, ch)]
        qks = [kq_[c:] * x["decay"] for kq_, x in zip(kkqk, ch)]
        t_invs = _unit_lower_inverse_many(ms, c)
        sols = [_mm(t, x["rhs"]) for t, x in zip(t_invs, ch)]
        ws_qs = [_mm(jnp.concatenate([sol[:, hd:], x["q_dec"]], axis=0), x["s_prev"])
                 for sol, x in zip(sols, ch)]
        v_news = [sol[:, :hd] - wq[:c] for sol, wq in zip(sols, ws_qs)]
        if c >= LANES:
            rs = [_mm(jnp.concatenate([qk, x["k_dec"].T], axis=0), vn)
                  for qk, x, vn in zip(qks, ch, v_news)]
            os_ = [wq[c:] + r[:c] for wq, r in zip(ws_qs, rs)]
            upd = [r[c:] for r in rs]
        else:
            os_ = [wq[c:] + _mm(qk, vn) for wq, qk, vn in zip(ws_qs, qks, v_news)]
            upd = [_mm_tn(x["k_dec"], vn) for x, vn in zip(ch, v_news)]
        states = [x["s_prev"] * jnp.exp(x["g_last"]) + u_ for x, u_ in zip(ch, upd)]
        outs = [o * lax.rsqrt(jnp.mean(o * o, axis=-1, keepdims=True) + EPS) * gdn * _silu(x["z"])
                for o, x in zip(os_, ch)]
        return outs, states

    def store_seq(b, ya, og, new_ha, new_hq, new_states):
        ya_ref[b] = ya
        og_ref[b] = og
        halo_a[b] = new_ha
        halo_q[b] = new_hq
        for h in range(heads):
            state[b, h] = new_states[h]

    def per_group(gi, carry):
        b0 = gi * unroll
        preps = [prep_seq(*load_seq(b0 + j)) for j in range(unroll)]
        outs, states = delta_chains([x for pr in preps for x in pr[3]])
        for j, (ya, new_ha, new_hq, _) in enumerate(preps):
            og = jnp.concatenate(outs[j * heads:(j + 1) * heads], axis=-1)
            store_seq(b0 + j, ya, og, new_ha, new_hq, states[j * heads:(j + 1) * heads])
        return carry

    lax.fori_loop(0, bb // unroll, per_group, 0)

    @pl.when(ci == n_chunks - 1)
    def _():
        sout_ref[...] = state[...]
        last = t_real - (n_chunks - 1) * c
        nbufa_ref[...] = (bch_ref[:, last - (ka - 1):last, cw:2 * cw]
                          * bch_ref[:, last - (ka - 1):last, 2 * cw:])
        nbufq_ref[...] = qkv_ref[:, last - (kq - 1):last, :]


def _mixer(proj3, buf_a, buf_q, s0, w_conv_a, w_conv_q, a_log_pad, dt_pad, g_dn, *, bb, c, t_real, heads, hd,
           unroll):
    bsz, tp, _ = proj3.shape
    cw = heads * hd
    n_chunks = tp // c
    ka, kq = w_conv_a.shape[0], w_conv_q.shape[0]
    kern = functools.partial(_mixer_kernel, bb=bb, c=c, heads=heads, hd=hd, t_real=t_real, n_chunks=n_chunks,
                             unroll=unroll)
    return pl.pallas_call(
        kern,
        grid=(bsz // bb, n_chunks),
        in_specs=[
            pl.BlockSpec((bb, c, 3 * cw), lambda i, j: (i, j, 0)),
            pl.BlockSpec((bb, c, 3 * cw), lambda i, j: (i, j, 1)),
            pl.BlockSpec((bb, c, cw), lambda i, j: (i, j, (3 * cw + 3 * cw + 4 * cw) // cw)),
            pl.BlockSpec((bb, c, LANES), lambda i, j: (i, j, (3 * cw + 3 * cw + 4 * cw + cw) // LANES)),
            pl.BlockSpec((bb, ka - 1, cw), lambda i, j: (i, 0, 0)),
            pl.BlockSpec((bb, kq - 1, 3 * cw), lambda i, j: (i, 0, 0)),
            pl.BlockSpec((bb, heads, hd, hd), lambda i, j: (i, 0, 0, 0)),
            pl.BlockSpec((ka, cw), lambda i, j: (0, 0)),
            pl.BlockSpec((kq, 3 * cw), lambda i, j: (0, 0)),
            pl.BlockSpec((1, LANES), lambda i, j: (0, 0)),
            pl.BlockSpec((1, LANES), lambda i, j: (0, 0)),
            pl.BlockSpec((1, hd), lambda i, j: (0, 0)),
        ],
        out_specs=[
            pl.BlockSpec((bb, c, cw), lambda i, j: (i, j, 0)),
            pl.BlockSpec((bb, c, cw), lambda i, j: (i, j, 0)),
            pl.BlockSpec((bb, ka - 1, cw), lambda i, j: (i, 0, 0)),
            pl.BlockSpec((bb, kq - 1, 3 * cw), lambda i, j: (i, 0, 0)),
            pl.BlockSpec((bb, heads, hd, hd), lambda i, j: (i, 0, 0, 0)),
        ],
        out_shape=[
            jax.ShapeDtypeStruct((bsz, tp, cw), F32),
            jax.ShapeDtypeStruct((bsz, tp, cw), F32),
            jax.ShapeDtypeStruct((bsz, ka - 1, cw), F32),
            jax.ShapeDtypeStruct((bsz, kq - 1, 3 * cw), F32),
            jax.ShapeDtypeStruct((bsz, heads, hd, hd), F32),
        ],
        scratch_shapes=[
            pltpu.VMEM((bb, SUBLANES, cw), F32),
            pltpu.VMEM((bb, SUBLANES, 3 * cw), F32),
            pltpu.VMEM((bb, heads, hd, hd), F32),
        ],
        compiler_params=_cparams(("parallel", "arbitrary")),
        name="mixer",
    )(proj3, proj3, proj3, proj3, buf_a, buf_q, s0, w_conv_a, w_conv_q, a_log_pad, dt_pad, g_dn)


def _out_proj_kernel(x_ref, ya_ref, og_ref, ga_ref, gb_ref, woa_ref, wod_ref, wo_ref, h_ref):
    y_a = jnp.dot(ya_ref[...].astype(BF16), woa_ref[...], preferred_element_type=F32)
    y_b = jnp.dot(og_ref[...].astype(BF16), wod_ref[...], preferred_element_type=F32)
    m = _sigmoid(ga_ref[...]) * y_a + _sigmoid(gb_ref[...]) * y_b
    h_ref[...] = x_ref[...] + jnp.dot(m.astype(BF16), wo_ref[...], preferred_element_type=F32)


def _out_proj(x2d, ya, og, proj, w_out_a, w_out_dn, w_o, tm, gate_block):
    n, d = x2d.shape
    cw = ya.shape[1]
    return pl.pallas_call(
        _out_proj_kernel,
        grid=(n // tm,),
        in_specs=[
            pl.BlockSpec((tm, d), lambda i: (i, 0)),
            pl.BlockSpec((tm, cw), lambda i: (i, 0)),
            pl.BlockSpec((tm, cw), lambda i: (i, 0)),
            pl.BlockSpec((tm, d), lambda i: (i, gate_block)),
            pl.BlockSpec((tm, d), lambda i: (i, gate_block + 1)),
            pl.BlockSpec((cw, d), lambda i: (0, 0)),
            pl.BlockSpec((cw, d), lambda i: (0, 0)),
            pl.BlockSpec((d, d), lambda i: (0, 0)),
        ],
        out_specs=pl.BlockSpec((tm, d), lambda i: (i, 0)),
        out_shape=jax.ShapeDtypeStruct((n, d), F32),
        compiler_params=_cparams(("parallel",)),
        name="out_proj",
    )(x2d, ya, og, proj, proj, w_out_a, w_out_dn, w_o)


def _extract_topk(ss, flat, topk, exact_ties):
    n, t = ss[0].shape
    big = jnp.int32(2 ** 30)
    ro = lax.broadcasted_iota(jnp.int32, (topk, t), 0)
    cur = list(ss)
    ranks = [jnp.full((n, t), float(topk), F32) for _ in ss]
    vals = [jnp.zeros((topk, t), F32) for _ in ss]
    for r in range(topk):
        for i in range(len(ss)):
            m = jnp.max(cur[i], axis=0, keepdims=True)
            hit = cur[i] == m
            if exact_ties:
                hit = flat == jnp.min(jnp.where(hit, flat, big), axis=0, keepdims=True)
            ranks[i] = jnp.where(hit, float(r), ranks[i])
            cur[i] = jnp.where(hit, NEG_INF, cur[i])
            vals[i] = jnp.where(ro == r, m, vals[i])
    return vals, ranks


def _topk_many(ss, flat, topk):
    vals, ranks = _extract_topk(ss, flat, topk, exact_ties=False)
    out = []
    for s_, v, rk in zip(ss, vals, ranks):
        n_marked = jnp.sum(jnp.where(rk < topk, 1.0, 0.0), axis=0, keepdims=True)
        no_ties = jnp.max(jnp.abs(n_marked - topk)) == 0.0

        def redo(s_=s_):
            v2, r2 = _extract_topk([s_], flat, topk, exact_ties=True)
            return v2[0], r2[0]

        out.append(lax.cond(no_ties, lambda v=v, rk=rk: (v, rk), redo))
    return out


def _pair_candidates(v1, v2, topk):
    io8 = lax.broadcasted_iota(jnp.int32, (SUBLANES, 1), 0)
    pieces = [v1[0:1] + v2]
    flats = [lax.broadcasted_iota(jnp.int32, (topk, 1), 0)]
    for a in range(1, SUBLANES):
        pieces.append(jnp.where(io8 < (topk // (a + 1)), v1[a:a + 1] + v2[0:SUBLANES], NEG_INF))
        flats.append(a * topk + io8)
    pieces.append(v1[SUBLANES:] + v2[0:1])
    flats.append((SUBLANES + io8) * topk)
    return jnp.concatenate(pieces, axis=0), jnp.concatenate(flats, axis=0)


def _pair_counts(sums, rank, mtot, topk):
    t = sums.shape[1]
    sel = jnp.where(rank < topk, 1.0, 0.0)
    z = jnp.sum(sel * jnp.exp(jnp.where(rank < topk, sums, mtot) - mtot), axis=0, keepdims=True)
    ro = lax.broadcasted_iota(jnp.int32, (topk, t), 0)
    nb = jnp.zeros((topk, t), F32)
    nb = jnp.where(ro == 0, jnp.sum(sel[0:topk], axis=0, keepdims=True), nb)
    for a in range(1, SUBLANES):
        lo = topk + (a - 1) * SUBLANES
        nb = jnp.where(ro == a, jnp.sum(sel[lo:lo + SUBLANES], axis=0, keepdims=True), nb)
    tail = sel[topk + (SUBLANES - 1) * SUBLANES:]
    nb = jnp.where(ro >= SUBLANES, jnp.concatenate([tail, tail], axis=0), nb)
    return nb, z


def _peer_select_kernel(h_ref, g_ref, wqt_ref, keys_ref, xn_ref, r2_ref, e2_ref, cnt_ref, c1_ref, q_scr,
                        *, heads, half, topk, group):
    xn = _rmsnorm(h_ref[...], g_ref[...]).astype(BF16)
    xn_ref[...] = pltpu.bitcast(xn, jnp.uint32)
    q_scr[...] = lax.dot_general(wqt_ref[...], xn, (((1,), (1,)), ((), ())), preferred_element_type=F32)
    nkeys = keys_ref.shape[1]
    key_idx = lax.broadcasted_iota(jnp.int32, (nkeys, 1), 0)

    def per_group(gi, carry):
        hs = [gi * group + j for j in range(group)]
        scores = []
        for h in hs:
            base = pl.multiple_of(h * 2 * half, 2 * half)
            scores.append(_mm(keys_ref[h], q_scr[pl.ds(base, half), :]))
            scores.append(_mm(keys_ref[heads + h], q_scr[pl.ds(base + half, half), :]))
        tops = _topk_many(scores, key_idx, topk)
        cands = [_pair_candidates(tops[2 * j][0], tops[2 * j + 1][0], topk) for j in range(group)]
        pair_tops = _topk_many([cd[0] for cd in cands], cands[0][1], topk)
        for j, h in enumerate(hs):
            (v1, r1), (v2, r2) = tops[2 * j], tops[2 * j + 1]
            nb, z = _pair_counts(cands[j][0], pair_tops[j][1], v1[0:1] + v2[0:1], topk)
            cnt = jnp.zeros(r1.shape, F32)
            for a in range(topk):
                cnt = jnp.where(r1 == float(a), nb[a:a + 1], cnt)
            r2_ref[h] = pltpu.bitcast(r2.astype(BF16), jnp.uint32)
            e2_ref[h] = pltpu.bitcast(jnp.exp(scores[2 * j + 1] - v2[0:1]).astype(BF16), jnp.uint32)
            cnt_ref[h] = cnt
            c1_ref[h] = jnp.exp(scores[2 * j] - v1[0:1]) / z
        return carry

    lax.fori_loop(0, heads // group, per_group, 0)


def _peer_select(h2d, g, wq_t, keys, *, heads, nkeys, half, topk, tms):
    n, d = h2d.shape
    qd = wq_t.shape[0]
    kern = functools.partial(_peer_select_kernel, heads=heads, half=half, topk=topk, group=2)
    sel_f32 = jax.ShapeDtypeStruct((heads, nkeys, n), F32)
    sel_pk = jax.ShapeDtypeStruct((heads, nkeys // 2, n), jnp.uint32)
    sel_spec = pl.BlockSpec((heads, nkeys, tms), lambda i: (0, 0, i))
    pk_spec = pl.BlockSpec((heads, nkeys // 2, tms), lambda i: (0, 0, i))
    return pl.pallas_call(
        kern,
        grid=(n // tms,),
        in_specs=[
            pl.BlockSpec((tms, d), lambda i: (i, 0)),
            pl.BlockSpec((1, d), lambda i: (0, 0)),
            pl.BlockSpec((qd, d), lambda i: (0, 0)),
            pl.BlockSpec(keys.shape, lambda i: (0, 0, 0)),
        ],
        out_specs=[pl.BlockSpec((tms // 2, d), lambda i: (i, 0)), pk_spec, pk_spec, sel_spec, sel_spec],
        out_shape=[jax.ShapeDtypeStruct((n // 2, d), jnp.uint32), sel_pk, sel_pk, sel_f32, sel_f32],
        scratch_shapes=[pltpu.VMEM((qd, tms), F32)],
        compiler_params=_cparams(("parallel",)),
        name="peer_select",
    )(h2d, g, wq_t, keys)


def _pair_index(s, offset, n_pairs):
    return jnp.clip(s - offset, 0, n_pairs - 1)


def _peer_dense_kernel(xn_ref, u_ref, vt_ref, r2_ref, e2_ref, cnt_ref, c1_ref, h_ref, g_ref, y_ref,
                       acc_ref, ht0_ref, ht1_ref, w0_ref, w1_ref, *, heads, nkeys, eb, tm, n_eb, n_pairs):
    s = pl.program_id(0)
    eb_b = lax.rem(_pair_index(s, 1, n_pairs), n_eb)
    eb_c = lax.rem(_pair_index(s, 2, n_pairs), n_eb)

    @pl.when(s == 0)
    def _():
        ht1_ref[...] = jnp.zeros(ht1_ref.shape, F32)
        w0_ref[...] = jnp.zeros(w0_ref.shape, jnp.uint32)
        w1_ref[...] = jnp.zeros(w1_ref.shape, jnp.uint32)

    @pl.when(eb_c == 0)
    def _():
        acc_ref[...] = jnp.zeros(acc_ref.shape, F32)

    i0 = pl.multiple_of(eb_b * (eb // nkeys), SUBLANES)
    zero = jnp.zeros((nkeys, LANES), BF16)

    def step(ht_cur, ht_prev, w_cur, w_prev):
        d = acc_ref.shape[0]
        xn = pltpu.bitcast(xn_ref[...], BF16)
        w_in = pltpu.bitcast(w_prev[...], BF16)
        for q in range(MM_PIECES):
            dr = slice(q * (d // MM_PIECES), (q + 1) * (d // MM_PIECES))
            dr2 = slice(q * (d // 2 // MM_PIECES), (q + 1) * (d // 2 // MM_PIECES))
            acc_ref[dr, :] += jnp.dot(pltpu.bitcast(vt_ref[dr2, :], BF16), w_in, preferred_element_type=F32)
            rows = eb // nkeys // MM_PIECES
            for ii in range(q * rows, (q + 1) * rows):
                rs = slice(ii * nkeys, (ii + 1) * nkeys)
                rs2 = slice(ii * (nkeys // 2), (ii + 1) * (nkeys // 2))
                for lc in range(tm // LANES):
                    ls = slice(lc * LANES, (lc + 1) * LANES)
                    gate = zero
                    for h in range(heads):
                        cnt = cnt_ref[h, pl.ds(i0, SUBLANES), ls][ii:ii + 1]
                        c1 = c1_ref[h, pl.ds(i0, SUBLANES), ls][ii:ii + 1]
                        cnt = jnp.broadcast_to(cnt, (nkeys, LANES)).astype(BF16)
                        c1 = jnp.broadcast_to(c1, (nkeys, LANES)).astype(BF16)
                        r2 = pltpu.bitcast(r2_ref[h, :, ls], BF16)
                        e2 = pltpu.bitcast(e2_ref[h, :, ls], BF16)
                        gate = gate + jnp.where(r2 < cnt, e2, zero) * c1
                    act = _gelu_exact(ht_prev[rs, ls]).astype(BF16)
                    w_cur[rs2, ls] = pltpu.bitcast(gate * act, jnp.uint32)
            er = slice(q * (eb // MM_PIECES), (q + 1) * (eb // MM_PIECES))
            er2 = slice(q * (eb // 2 // MM_PIECES), (q + 1) * (eb // 2 // MM_PIECES))
            ht_cur[er, :] = lax.dot_general(pltpu.bitcast(u_ref[er2, :], BF16), xn, (((1,), (1,)), ((), ())),
                                            preferred_element_type=F32)

    parity = lax.rem(s, 2)

    @pl.when(parity == 0)
    def _():
        step(ht0_ref, ht1_ref, w0_ref, w1_ref)

    @pl.when(parity == 1)
    def _():
        step(ht1_ref, ht0_ref, w1_ref, w0_ref)

    @pl.when(jnp.logical_and(s >= 2, eb_c == n_eb - 1))
    def _():
        y_ref[...] = _rmsnorm(h_ref[...] + acc_ref[...].T, g_ref[...])


def _peer_dense(xn, u, vt, r2, e2, cnt, c1, h2d, g, *, heads, nkeys, eb, tm):
    n, d = h2d.shape
    n_exp = 2 * u.shape[0]
    assert eb == SUBLANES * nkeys
    n_eb = n_exp // eb
    n_pairs = (n // tm) * n_eb
    kern = functools.partial(_peer_dense_kernel, heads=heads, nkeys=nkeys, eb=eb, tm=tm, n_eb=n_eb, n_pairs=n_pairs)

    def tb(offset):
        return lambda s: _pair_index(s, offset, n_pairs) // n_eb

    def ebk(offset):
        return lambda s: lax.rem(_pair_index(s, offset, n_pairs), n_eb)

    sel_spec = pl.BlockSpec((heads, nkeys, tm), lambda s: (0, 0, tb(1)(s)))
    pk_spec = pl.BlockSpec((heads, nkeys // 2, tm), lambda s: (0, 0, tb(1)(s)))
    return pl.pallas_call(
        kern,
        grid=(n_pairs + 2,),
        in_specs=[
            pl.BlockSpec((tm // 2, d), lambda s: (tb(0)(s), 0)),
            pl.BlockSpec((eb // 2, d), lambda s: (ebk(0)(s), 0)),
            pl.BlockSpec((d // 2, eb), lambda s: (0, ebk(2)(s))),
            pk_spec, pk_spec, sel_spec, sel_spec,
            pl.BlockSpec((tm, d), lambda s: (tb(2)(s), 0)),
            pl.BlockSpec((1, d), lambda s: (0, 0)),
        ],
        out_specs=pl.BlockSpec((tm, d), lambda s: (tb(2)(s), 0)),
        out_shape=jax.ShapeDtypeStruct((n, d), F32),
        scratch_shapes=[
            pltpu.VMEM((d, tm), F32),
            pltpu.VMEM((eb, tm), F32),
            pltpu.VMEM((eb, tm), F32),
            pltpu.VMEM((eb // 2, tm), jnp.uint32),
            pltpu.VMEM((eb // 2, tm), jnp.uint32),
        ],
        compiler_params=_cparams(("arbitrary",)),
        name="peer_dense",
    )(xn, u, vt, r2, e2, cnt, c1, h2d, g)


def _pick_block(n, candidates):
    for c in candidates:
        if n % c == 0:
            return c
    raise ValueError(f"no block size in {candidates} divides {n}")


def _trunk(x, buf_a, buf_q, s0, t_real, wts, dims):
    heads, hd, cw, p_heads, nkeys, half = dims
    (g_mix, w_in, w_conv_a, w_conv_q, a_log_pad, dt_pad, g_dn, w_out_a, w_out_dn, w_o, g_ffn,
     wq_t, keys, u_bf, vt_bf, g_final) = wts
    bsz, tp, d = x.shape
    n = bsz * tp
    x2d = x.reshape(n, d)
    c = DN_CHUNK if tp % DN_CHUNK == 0 else tp
    ka, kq = w_conv_a.shape[0], w_conv_q.shape[0]
    last = t_real - (tp // c - 1) * c
    assert 0 < last <= c and last >= kq - 1 and last >= ka - 1, (t_real, tp, c)

    tm = _pick_block(n, (768, 512, 256, 128))
    proj = _in_proj(x2d, g_mix, w_in, tm, w_in.shape[1] // 5)
    if c == DN_CHUNK:
        bb, unroll = _pick_block(bsz, (4, 2, 1)), 2
    else:
        bb, unroll = _pick_block(bsz, (16, 8, 4, 2, 1)), 16
    ya, og, nbuf_a, nbuf_q, s_new = _mixer(
        proj.reshape(bsz, tp, -1), buf_a, buf_q, s0, w_conv_a, w_conv_q, a_log_pad, dt_pad, g_dn,
        bb=bb, c=c, t_real=t_real, heads=heads, hd=hd, unroll=min(unroll, bb))
    tm2 = _pick_block(n, (512, 256, 128))
    hres = _out_proj(x2d, ya.reshape(n, cw), og.reshape(n, cw), proj, w_out_a, w_out_dn, w_o, tm2,
                     gate_block=(6 * cw) // d)
    xn, r2, e2, cnt, c1 = _peer_select(hres, g_ffn, wq_t, keys, heads=p_heads, nkeys=nkeys, half=half,
                                       topk=PEER_TOPK, tms=LANES)
    tm3 = _pick_block(n, (512, 256, 128))
    eb = _pick_block(2 * u_bf.shape[0], (1024, 512, 256, 128))
    y = _peer_dense(xn, u_bf, vt_bf, r2, e2, cnt, c1, hres, g_final, heads=p_heads, nkeys=nkeys, eb=eb, tm=tm3)
    return y.reshape(bsz, tp, d), nbuf_a, nbuf_q, s_new


def kernel(x_prompt, x_sample, state_conv_a, state_conv_qkv, state_delta, g_norm_mix, w_in, w_conv_a, w_conv_qkv,
           a_log, dt_bias, g_dn_norm, w_out_a, w_out_dn, w_o, g_norm_ffn, w_query, sub_keys, expert_u, expert_v,
           g_norm_final):
    depth = w_in.shape[0]
    assert depth == 1
    d = x_prompt.shape[-1]
    heads = a_log.shape[1]
    hd = g_dn_norm.shape[1]
    cw = heads * hd
    assert w_conv_a.shape[2] == cw and w_conv_qkv.shape[2] == 3 * cw and d == 2 * cw
    p_heads, nkeys, half = sub_keys.shape[2], sub_keys.shape[3], sub_keys.shape[4]
    assert nkeys == LANES and expert_u.shape[1] == nkeys * nkeys

    w = w_in[0]
    o_qkv, o_z, o_a, o_b, o_g = 3 * cw, 6 * cw, 7 * cw, 7 * cw + heads, 7 * cw + 2 * heads
    ab_pad = jnp.zeros((d, LANES - 2 * heads), w.dtype)
    w_r = jnp.concatenate([w[:, :o_z], w[:, o_g:], w[:, o_z:o_a], w[:, o_a:o_g], ab_pad], axis=1).astype(BF16)
    pad4 = lambda v: jnp.pad(v.astype(F32), ((0, 0), (0, LANES - v.shape[1])))
    wts = (
        g_norm_mix, w_r, w_conv_a[0], w_conv_qkv[0], pad4(a_log), pad4(dt_bias), g_dn_norm,
        w_out_a[0].astype(BF16), w_out_dn[0].astype(BF16), w_o[0].astype(BF16), g_norm_ffn,
        w_query[0].T.astype(BF16), sub_keys[0].reshape(2 * p_heads, nkeys, half).astype(BF16),
        _pack_rows(expert_u[0].astype(BF16)), _pack_rows(expert_v[0].T.astype(BF16)), g_norm_final.reshape(1, d),
    )
    dims = (heads, hd, cw, p_heads, nkeys, half)

    bp, tpr, _ = x_prompt.shape
    assert tpr % DN_CHUNK == 0
    ka, kq = w_conv_a.shape[1], w_conv_qkv.shape[1]
    zero_a = jnp.zeros((bp, ka - 1, cw), F32)
    zero_q = jnp.zeros((bp, kq - 1, 3 * cw), F32)
    zero_s = jnp.zeros((bp, heads, hd, hd), F32)
    y_p, a_p, q_p, s_p = _trunk(x_prompt, zero_a, zero_q, zero_s, tpr, wts, dims)

    bs, ts, _ = x_sample.shape
    assert ts <= SUBLANES
    xs = jnp.pad(x_sample, ((0, 0), (0, SUBLANES - ts), (0, 0)))
    y_s, a_s, q_s, s_s = _trunk(xs, state_conv_a[0], state_conv_qkv[0], state_delta[0], ts, wts, dims)
    y_s = y_s[:, :ts]
    return (y_p, y_s, a_p[None], q_p[None], s_p[None], a_s[None], q_s[None], s_s[None])
```

```python
import functools

import jax
import jax.numpy as jnp
from jax import lax
from jax.experimental import pallas as pl
from jax.experimental.pallas import tpu as pltpu

EPS = 1e-6
F32 = jnp.float32
BF16 = jnp.bfloat16
LANES = 128
SUBLANES = 8
VMEM_LIMIT = 56 * 1024 * 1024

PEER_TOPK = 16
DN_CHUNK = 128
MM_PIECES = 8
NEG_INF = float("-inf")


def _cparams(sem):
    return pltpu.CompilerParams(dimension_semantics=sem, vmem_limit_bytes=VMEM_LIMIT)


def _mm(a, b):
    return jnp.dot(a.astype(BF16), b.astype(BF16), preferred_element_type=F32)


def _mm_nt(a, b):
    return lax.dot_general(a.astype(BF16), b.astype(BF16), (((1,), (1,)), ((), ())), preferred_element_type=F32)


def _mm_tn(a, b):
    return lax.dot_general(a.astype(BF16), b.astype(BF16), (((0,), (0,)), ((), ())), preferred_element_type=F32)


def _rmsnorm(x, g):
    return x * lax.rsqrt(jnp.mean(x * x, axis=-1, keepdims=True) + EPS) * g


def _sigmoid(x):
    return 1.0 / (1.0 + jnp.exp(-x))


def _silu(x):
    return x * _sigmoid(x)


def _softplus(x):
    return jnp.maximum(x, 0.0) + jnp.log(1.0 + jnp.exp(-jnp.abs(x)))


def _gelu_exact(x):
    return 0.5 * x * (1.0 + lax.erf(x * (2.0 ** -0.5)))


def _in_proj_kernel(x_ref, g_ref, w_ref, o_ref):
    xn = _rmsnorm(x_ref[...], g_ref[...]).astype(BF16)
    o_ref[...] = jnp.dot(xn, w_ref[...], preferred_element_type=F32)


def _in_proj(x2d, g, w, tm):
    n, d = x2d.shape
    width = w.shape[1]
    return pl.pallas_call(
        _in_proj_kernel,
        grid=(n // tm,),
        in_specs=[
            pl.BlockSpec((tm, d), lambda i: (i, 0)),
            pl.BlockSpec((1, d), lambda i: (0, 0)),
            pl.BlockSpec((d, width), lambda i: (0, 0)),
        ],
        out_specs=pl.BlockSpec((tm, width), lambda i: (i, 0)),
        out_shape=jax.ShapeDtypeStruct((n, width), F32),
        compiler_params=_cparams(("parallel",)),
        name="in_proj",
    )(x2d, g, w)


def _shift_rows(xp, k, rows):
    if k == 0:
        return xp[SUBLANES:SUBLANES + rows]
    return pltpu.roll(xp, k, axis=0)[SUBLANES:SUBLANES + rows]


def _cumsum_rows(x, rows):
    ridx = lax.broadcasted_iota(jnp.int32, x.shape, 0)
    s = 1
    while s < rows:
        x = x + jnp.where(ridx >= s, pltpu.roll(x, s, axis=0), 0.0)
        s *= 2
    return x


def _unit_lower_inverse_many(ms, c):
    eye = (lax.broadcasted_iota(jnp.int32, (c, c), 0) == lax.broadcasted_iota(jnp.int32, (c, c), 1)).astype(F32)
    n_factors = c.bit_length() - 1
    assert 1 << n_factors == c
    ps = [-m for m in ms]
    ts = [eye + p for p in ps]
    if n_factors == 1:
        return ts
    ps = [_mm(p, p) for p in ps]
    for _ in range(n_factors - 2):
        tps = [_mm(jnp.concatenate([t, p], axis=0), p) for t, p in zip(ts, ps)]
        ts = [t + tp[:c] for t, tp in zip(ts, tps)]
        ps = [tp[c:] for tp in tps]
    return [t + _mm(t, p) for t, p in zip(ts, ps)]


def _mixer_kernel(bch_ref, qkv_ref, z_ref, ab_ref, bufa_ref, bufq_ref, s0_ref,
                  wca_ref, wcq_ref, alog_ref, dtb_ref, gdn_ref,
                  ya_ref, og_ref, nbufa_ref, nbufq_ref, sout_ref,
                  halo_a, halo_q, state,
                  *, bb, c, heads, hd, t_real, n_chunks, unroll):
    ci = pl.program_id(1)
    cw = heads * hd
    ka = wca_ref.shape[0]
    kq = wcq_ref.shape[0]

    @pl.when(ci == 0)
    def _():
        state[...] = s0_ref[...]
        halo_a[...] = jnp.zeros(halo_a.shape, F32)
        halo_q[...] = jnp.zeros(halo_q.shape, F32)
        halo_a[:, SUBLANES - (ka - 1):, :] = bufa_ref[...]
        halo_q[:, SUBLANES - (kq - 1):, :] = bufq_ref[...]

    wca = wca_ref[...]
    wcq = wcq_ref[...]
    neg_a = -jnp.exp(alog_ref[...])
    dtb = dtb_ref[...]
    gdn = gdn_ref[...]
    rows_left = t_real - ci * c
    ridx = lax.broadcasted_iota(jnp.int32, (c, 1), 0)
    valid = ridx < rows_left
    ri = lax.broadcasted_iota(jnp.int32, (c, c), 0)
    si = lax.broadcasted_iota(jnp.int32, (c, c), 1)
    causal = ri >= si
    strict = ri > si
    cpad = max(c, LANES)

    rb = bch_ref.shape[1]

    def pad_rows(x):
        if rb == c:
            return x
        return jnp.concatenate([x, jnp.zeros((c - rb, x.shape[1]), x.dtype)], axis=0)

    def load_seq(b):
        return (pad_rows(bch_ref[b]), pad_rows(qkv_ref[b]), pad_rows(z_ref[b]), pad_rows(ab_ref[b]),
                halo_a[b], halo_q[b], [state[b, h] for h in range(heads)])

    def prep_seq(bch, qkv, z, ab, ha, hq, s_prev_all):
        b_c, c_c, h_c = bch[:, :cw], bch[:, cw:2 * cw], bch[:, 2 * cw:]
        ta = c_c * h_c
        xpa = jnp.concatenate([ha, ta], axis=0)
        conv_a = wca[ka - 1:ka] * _shift_rows(xpa, 0, c)
        for j in range(1, ka):
            conv_a = conv_a + wca[ka - 1 - j:ka - j] * _shift_rows(xpa, j, c)
        ya = b_c * conv_a
        xpq = jnp.concatenate([hq, qkv], axis=0)
        conv_q = wcq[kq - 1:kq] * _shift_rows(xpq, 0, c)
        for j in range(1, kq):
            conv_q = conv_q + wcq[kq - 1 - j:kq - j] * _shift_rows(xpq, j, c)
        conv_q = _silu(conv_q)
        new_ha, new_hq = xpa[c:c + SUBLANES], xpq[c:c + SUBLANES]
        g_all = jnp.where(valid, neg_a * _softplus(ab + dtb), 0.0)
        beta_all = jnp.where(valid, _sigmoid(ab), 0.0)
        gcum = _cumsum_rows(g_all, c)
        if c < cpad:
            gpad = jnp.concatenate([gcum, jnp.zeros((cpad - c, LANES), F32)], axis=0)
        else:
            gpad = gcum
        gcum_t = gpad.T
        chains = []
        for h in range(heads):
            q = conv_q[:, h * hd:(h + 1) * hd]
            k = conv_q[:, cw + h * hd:cw + (h + 1) * hd]
            v = conv_q[:, 2 * cw + h * hd:2 * cw + (h + 1) * hd]
            q = q * lax.rsqrt(jnp.sum(q * q, axis=-1, keepdims=True) + EPS) * (hd ** -0.5)
            k = k * lax.rsqrt(jnp.sum(k * k, axis=-1, keepdims=True) + EPS)
            q = jnp.where(valid, q, 0.0)
            k = jnp.where(valid, k, 0.0)
            v = jnp.where(valid, v, 0.0)
            gc = gcum[:, h:h + 1]
            gr = gcum_t[h:h + 1, :c]
            beta = beta_all[:, heads + h:heads + h + 1]
            g_last = gcum[c - 1:c, h:h + 1]
            eg = jnp.exp(gc)
            chains.append(dict(
                q=q, k=k, beta=beta, g_last=g_last, s_prev=s_prev_all[h], z=z[:, h * hd:(h + 1) * hd],
                decay=jnp.exp(jnp.where(causal, gc - gr, NEG_INF)),
                rhs=jnp.concatenate([v * beta, k * (beta * eg)], axis=-1),
                q_dec=q * eg, k_dec=k * jnp.exp(g_last - gc)))
        return ya, new_ha, new_hq, chains

    def delta_chains(ch):
        kkqk = [_mm_nt(jnp.concatenate([x["k"], x["q"]], axis=0), x["k"]) for x in ch]
        ms = [jnp.where(strict, kq_[:c] * x["decay"] * x["beta"], 0.0) for kq_, x in zip(kkqk, ch)]
        qks = [kq_[c:] * x["decay"] for kq_, x in zip(kkqk, ch)]
        t_invs = _unit_lower_inverse_many(ms, c)
        sols = [_mm(t, x["rhs"]) for t, x in zip(t_invs, ch)]
        ws_qs = [_mm(jnp.concatenate([sol[:, hd:], x["q_dec"]], axis=0), x["s_prev"])
                 for sol, x in zip(sols, ch)]
        v_news = [sol[:, :hd] - wq[:c] for sol, wq in zip(sols, ws_qs)]
        if c >= LANES:
            rs = [_mm(jnp.concatenate([qk, x["k_dec"].T], axis=0), vn)
                  for qk, x, vn in zip(qks, ch, v_news)]
            os_ = [wq[c:] + r[:c] for wq, r in zip(ws_qs, rs)]
            upd = [r[c:] for r in rs]
        else:
            os_ = [wq[c:] + _mm(qk, vn) for wq, qk, vn in zip(ws_qs, qks, v_news)]
            upd = [_mm_tn(x["k_dec"], vn) for x, vn in zip(ch, v_news)]
        states = [x["s_prev"] * jnp.exp(x["g_last"]) + u_ for x, u_ in zip(ch, upd)]
        outs = [o * lax.rsqrt(jnp.mean(o * o, axis=-1, keepdims=True) + EPS) * gdn * _silu(x["z"])
                for o, x in zip(os_, ch)]
        return outs, states

    def store_seq(b, ya, og, new_ha, new_hq, new_states):
        ya_ref[b] = ya[:rb]
        og_ref[b] = og[:rb]
        halo_a[b] = new_ha
        halo_q[b] = new_hq
        for h in range(heads):
            state[b, h] = new_states[h]

    def per_group(gi, carry):
        b0 = gi * unroll
        preps = [prep_seq(*load_seq(b0 + j)) for j in range(unroll)]
        outs, states = delta_chains([x for pr in preps for x in pr[3]])
        for j, (ya, new_ha, new_hq, _) in enumerate(preps):
            og = jnp.concatenate(outs[j * heads:(j + 1) * heads], axis=-1)
            store_seq(b0 + j, ya, og, new_ha, new_hq, states[j * heads:(j + 1) * heads])
        return carry

    lax.fori_loop(0, bb // unroll, per_group, 0)

    @pl.when(ci == n_chunks - 1)
    def _():
        sout_ref[...] = state[...]
        last = t_real - (n_chunks - 1) * c
        nbufa_ref[...] = (bch_ref[:, last - (ka - 1):last, cw:2 * cw]
                          * bch_ref[:, last - (ka - 1):last, 2 * cw:])
        nbufq_ref[...] = qkv_ref[:, last - (kq - 1):last, :]


def _mixer(proj3, buf_a, buf_q, s0, w_conv_a, w_conv_q, a_log_pad, dt_pad, g_dn, *, bb, c, t_real, heads, hd,
           unroll):
    bsz, tp, _ = proj3.shape
    cw = heads * hd
    n_chunks = pl.cdiv(tp, c)
    rb = min(c, tp)
    assert n_chunks * rb == tp
    ka, kq = w_conv_a.shape[0], w_conv_q.shape[0]
    kern = functools.partial(_mixer_kernel, bb=bb, c=c, heads=heads, hd=hd, t_real=t_real, n_chunks=n_chunks,
                             unroll=unroll)
    return pl.pallas_call(
        kern,
        grid=(bsz // bb, n_chunks),
        in_specs=[
            pl.BlockSpec((bb, rb, 3 * cw), lambda i, j: (i, j, 0)),
            pl.BlockSpec((bb, rb, 3 * cw), lambda i, j: (i, j, 1)),
            pl.BlockSpec((bb, rb, cw), lambda i, j: (i, j, (3 * cw + 3 * cw + 4 * cw) // cw)),
            pl.BlockSpec((bb, rb, LANES), lambda i, j: (i, j, (3 * cw + 3 * cw + 4 * cw + cw) // LANES)),
            pl.BlockSpec((bb, ka - 1, cw), lambda i, j: (i, 0, 0)),
            pl.BlockSpec((bb, kq - 1, 3 * cw), lambda i, j: (i, 0, 0)),
            pl.BlockSpec((bb, heads, hd, hd), lambda i, j: (i, 0, 0, 0)),
            pl.BlockSpec((ka, cw), lambda i, j: (0, 0)),
            pl.BlockSpec((kq, 3 * cw), lambda i, j: (0, 0)),
            pl.BlockSpec((1, LANES), lambda i, j: (0, 0)),
            pl.BlockSpec((1, LANES), lambda i, j: (0, 0)),
            pl.BlockSpec((1, hd), lambda i, j: (0, 0)),
        ],
        out_specs=[
            pl.BlockSpec((bb, rb, cw), lambda i, j: (i, j, 0)),
            pl.BlockSpec((bb, rb, cw), lambda i, j: (i, j, 0)),
            pl.BlockSpec((bb, ka - 1, cw), lambda i, j: (i, 0, 0)),
            pl.BlockSpec((bb, kq - 1, 3 * cw), lambda i, j: (i, 0, 0)),
            pl.BlockSpec((bb, heads, hd, hd), lambda i, j: (i, 0, 0, 0)),
        ],
        out_shape=[
            jax.ShapeDtypeStruct((bsz, tp, cw), F32),
            jax.ShapeDtypeStruct((bsz, tp, cw), F32),
            jax.ShapeDtypeStruct((bsz, ka - 1, cw), F32),
            jax.ShapeDtypeStruct((bsz, kq - 1, 3 * cw), F32),
            jax.ShapeDtypeStruct((bsz, heads, hd, hd), F32),
        ],
        scratch_shapes=[
            pltpu.VMEM((bb, SUBLANES, cw), F32),
            pltpu.VMEM((bb, SUBLANES, 3 * cw), F32),
            pltpu.VMEM((bb, heads, hd, hd), F32),
        ],
        compiler_params=_cparams(("parallel", "arbitrary")),
        name="mixer",
    )(proj3, proj3, proj3, proj3, buf_a, buf_q, s0, w_conv_a, w_conv_q, a_log_pad, dt_pad, g_dn)


def _out_proj_kernel(x_ref, ya_ref, og_ref, ga_ref, gb_ref, woa_ref, wod_ref, wo_ref, h_ref):
    y_a = jnp.dot(ya_ref[...].astype(BF16), woa_ref[...], preferred_element_type=F32)
    y_b = jnp.dot(og_ref[...].astype(BF16), wod_ref[...], preferred_element_type=F32)
    m = _sigmoid(ga_ref[...]) * y_a + _sigmoid(gb_ref[...]) * y_b
    h_ref[...] = x_ref[...] + jnp.dot(m.astype(BF16), wo_ref[...], preferred_element_type=F32)


def _out_proj(x2d, ya, og, proj, w_out_a, w_out_dn, w_o, tm, gate_block):
    n, d = x2d.shape
    cw = ya.shape[1]
    return pl.pallas_call(
        _out_proj_kernel,
        grid=(n // tm,),
        in_specs=[
            pl.BlockSpec((tm, d), lambda i: (i, 0)),
            pl.BlockSpec((tm, cw), lambda i: (i, 0)),
            pl.BlockSpec((tm, cw), lambda i: (i, 0)),
            pl.BlockSpec((tm, d), lambda i: (i, gate_block)),
            pl.BlockSpec((tm, d), lambda i: (i, gate_block + 1)),
            pl.BlockSpec((cw, d), lambda i: (0, 0)),
            pl.BlockSpec((cw, d), lambda i: (0, 0)),
            pl.BlockSpec((d, d), lambda i: (0, 0)),
        ],
        out_specs=pl.BlockSpec((tm, d), lambda i: (i, 0)),
        out_shape=jax.ShapeDtypeStruct((n, d), F32),
        compiler_params=_cparams(("parallel",)),
        name="out_proj",
    )(x2d, ya, og, proj, proj, w_out_a, w_out_dn, w_o)


def _pack_u_kernel(u_ref, o_ref):
    o_ref[...] = pltpu.bitcast(u_ref[...].astype(BF16), jnp.uint32)


def _pack_vt_kernel(v_ref, o_ref):
    o_ref[...] = pltpu.bitcast(v_ref[...].T.astype(BF16), jnp.uint32)


def _pack_experts(expert_u, expert_v, rows):
    n_exp, d = expert_u.shape
    u_pk = pl.pallas_call(
        _pack_u_kernel,
        grid=(n_exp // rows,),
        in_specs=[pl.BlockSpec((rows, d), lambda i: (i, 0))],
        out_specs=pl.BlockSpec((rows // 2, d), lambda i: (i, 0)),
        out_shape=jax.ShapeDtypeStruct((n_exp // 2, d), jnp.uint32),
        compiler_params=_cparams(("parallel",)),
        name="pack_u",
    )(expert_u)
    vt_pk = pl.pallas_call(
        _pack_vt_kernel,
        grid=(n_exp // rows,),
        in_specs=[pl.BlockSpec((rows, d), lambda i: (i, 0))],
        out_specs=pl.BlockSpec((d // 2, rows), lambda i: (0, i)),
        out_shape=jax.ShapeDtypeStruct((d // 2, n_exp), jnp.uint32),
        compiler_params=_cparams(("parallel",)),
        name="pack_vt",
    )(expert_v)
    return u_pk, vt_pk


def _extract_topk(ss, flat, topk, exact_ties):
    n, t = ss[0].shape
    big = jnp.int32(2 ** 30)
    ro = lax.broadcasted_iota(jnp.int32, (topk, t), 0)
    cur = list(ss)
    ranks = [jnp.full((n, t), float(topk), F32) for _ in ss]
    vals = [jnp.zeros((topk, t), F32) for _ in ss]
    for r in range(topk):
        for i in range(len(ss)):
            m = jnp.max(cur[i], axis=0, keepdims=True)
            hit = cur[i] == m
            if exact_ties:
                hit = flat == jnp.min(jnp.where(hit, flat, big), axis=0, keepdims=True)
            ranks[i] = jnp.where(hit, float(r), ranks[i])
            cur[i] = jnp.where(hit, NEG_INF, cur[i])
            vals[i] = jnp.where(ro == r, m, vals[i])
    return vals, ranks


def _topk_many(ss, flat, topk):
    vals, ranks = _extract_topk(ss, flat, topk, exact_ties=False)
    out = []
    for s_, v, rk in zip(ss, vals, ranks):
        n_marked = jnp.sum(jnp.where(rk < topk, 1.0, 0.0), axis=0, keepdims=True)
        no_ties = jnp.max(jnp.abs(n_marked - topk)) == 0.0

        def redo(s_=s_):
            v2, r2 = _extract_topk([s_], flat, topk, exact_ties=True)
            return v2[0], r2[0]

        out.append(lax.cond(no_ties, lambda v=v, rk=rk: (v, rk), redo))
    return out


def _pair_candidates(v1, v2, topk):
    io8 = lax.broadcasted_iota(jnp.int32, (SUBLANES, 1), 0)
    pieces = [v1[0:1] + v2]
    flats = [lax.broadcasted_iota(jnp.int32, (topk, 1), 0)]
    for a in range(1, SUBLANES):
        pieces.append(jnp.where(io8 < (topk // (a + 1)), v1[a:a + 1] + v2[0:SUBLANES], NEG_INF))
        flats.append(a * topk + io8)
    pieces.append(v1[SUBLANES:] + v2[0:1])
    flats.append((SUBLANES + io8) * topk)
    return jnp.concatenate(pieces, axis=0), jnp.concatenate(flats, axis=0)


def _pair_counts(sums, rank, mtot, topk):
    t = sums.shape[1]
    sel = jnp.where(rank < topk, 1.0, 0.0)
    z = jnp.sum(sel * jnp.exp(jnp.where(rank < topk, sums, mtot) - mtot), axis=0, keepdims=True)
    ro = lax.broadcasted_iota(jnp.int32, (topk, t), 0)
    nb = jnp.zeros((topk, t), F32)
    nb = jnp.where(ro == 0, jnp.sum(sel[0:topk], axis=0, keepdims=True), nb)
    for a in range(1, SUBLANES):
        lo = topk + (a - 1) * SUBLANES
        nb = jnp.where(ro == a, jnp.sum(sel[lo:lo + SUBLANES], axis=0, keepdims=True), nb)
    tail = sel[topk + (SUBLANES - 1) * SUBLANES:]
    nb = jnp.where(ro >= SUBLANES, jnp.concatenate([tail, tail], axis=0), nb)
    return nb, z


def _peer_select_kernel(h_ref, g_ref, wqt_ref, keys_ref, xn_ref, r2_ref, e2_ref, cnt_ref, c1_ref, q_scr,
                        *, heads, half, topk, group):
    xf = _rmsnorm(h_ref[...], g_ref[...])
    xn = xf.astype(BF16)
    xn_ref[...] = pltpu.bitcast(xf.T.astype(BF16), jnp.uint32)
    q_scr[...] = lax.dot_general(wqt_ref[...], xn, (((1,), (1,)), ((), ())), preferred_element_type=F32)
    nkeys = keys_ref.shape[1]
    key_idx = lax.broadcasted_iota(jnp.int32, (nkeys, 1), 0)

    def per_group(gi, carry):
        hs = [gi * group + j for j in range(group)]
        scores = []
        for h in hs:
            base = pl.multiple_of(h * 2 * half, 2 * half)
            scores.append(_mm(keys_ref[h], q_scr[pl.ds(base, half), :]))
            scores.append(_mm(keys_ref[heads + h], q_scr[pl.ds(base + half, half), :]))
        tops = _topk_many(scores, key_idx, topk)
        cands = [_pair_candidates(tops[2 * j][0], tops[2 * j + 1][0], topk) for j in range(group)]
        pair_tops = _topk_many([cd[0] for cd in cands], cands[0][1], topk)
        for j, h in enumerate(hs):
            (v1, r1), (v2, r2) = tops[2 * j], tops[2 * j + 1]
            nb, z = _pair_counts(cands[j][0], pair_tops[j][1], v1[0:1] + v2[0:1], topk)
            cnt = jnp.zeros(r1.shape, F32)
            for a in range(topk):
                cnt = jnp.where(r1 == float(a), nb[a:a + 1], cnt)
            r2_ref[h] = pltpu.bitcast(r2.astype(BF16), jnp.uint32)
            e2_ref[h] = pltpu.bitcast(jnp.exp(scores[2 * j + 1] - v2[0:1]).astype(BF16), jnp.uint32)
            cnt_ref[h] = cnt
            c1_ref[h] = jnp.exp(scores[2 * j] - v1[0:1]) / z
        return carry

    lax.fori_loop(0, heads // group, per_group, 0)


def _peer_select(h2d, g, wq_t, keys, *, heads, nkeys, half, topk, tms):
    n, d = h2d.shape
    qd = wq_t.shape[0]
    kern = functools.partial(_peer_select_kernel, heads=heads, half=half, topk=topk, group=2)
    sel_f32 = jax.ShapeDtypeStruct((heads, nkeys, n), F32)
    sel_pk = jax.ShapeDtypeStruct((heads, nkeys // 2, n), jnp.uint32)
    sel_spec = pl.BlockSpec((heads, nkeys, tms), lambda i: (0, 0, i))
    pk_spec = pl.BlockSpec((heads, nkeys // 2, tms), lambda i: (0, 0, i))
    return pl.pallas_call(
        kern,
        grid=(n // tms,),
        in_specs=[
            pl.BlockSpec((tms, d), lambda i: (i, 0)),
            pl.BlockSpec((1, d), lambda i: (0, 0)),
            pl.BlockSpec((qd, d), lambda i: (0, 0)),
            pl.BlockSpec(keys.shape, lambda i: (0, 0, 0)),
        ],
        out_specs=[pl.BlockSpec((d // 2, tms), lambda i: (0, i)), pk_spec, pk_spec, sel_spec, sel_spec],
        out_shape=[jax.ShapeDtypeStruct((d // 2, n), jnp.uint32), sel_pk, sel_pk, sel_f32, sel_f32],
        scratch_shapes=[pltpu.VMEM((qd, tms), F32)],
        compiler_params=_cparams(("parallel",)),
        name="peer_select",
    )(h2d, g, wq_t, keys)


def _pair_index(s, offset, n_pairs):
    return jnp.clip(s - offset, 0, n_pairs - 1)


def _peer_dense_kernel(xn_ref, u_ref, vt_ref, r2_ref, e2_ref, cnt_ref, c1_ref, h_ref, g_ref, y_ref,
                       acc_ref, ht0_ref, ht1_ref, w0_ref, w1_ref, *, heads, nkeys, eb, tm, n_eb, n_pairs):
    s = pl.program_id(0)
    eb_b = lax.rem(_pair_index(s, 1, n_pairs), n_eb)
    eb_c = lax.rem(_pair_index(s, 2, n_pairs), n_eb)

    @pl.when(s == 0)
    def _():
        ht1_ref[...] = jnp.zeros(ht1_ref.shape, F32)
        w0_ref[...] = jnp.zeros(w0_ref.shape, jnp.uint32)
        w1_ref[...] = jnp.zeros(w1_ref.shape, jnp.uint32)

    @pl.when(eb_c == 0)
    def _():
        acc_ref[...] = jnp.zeros(acc_ref.shape, F32)

    i0 = pl.multiple_of(eb_b * (eb // nkeys), SUBLANES)
    zero = jnp.zeros((nkeys, LANES), BF16)

    def step(ht_cur, ht_prev, w_cur, w_prev):
        d = acc_ref.shape[0]
        xn_t = pltpu.bitcast(xn_ref[...], BF16)
        w_in = pltpu.bitcast(w_prev[...], BF16)
        for q in range(MM_PIECES):
            dr = slice(q * (d // MM_PIECES), (q + 1) * (d // MM_PIECES))
            dr2 = slice(q * (d // 2 // MM_PIECES), (q + 1) * (d // 2 // MM_PIECES))
            acc_ref[dr, :] += jnp.dot(pltpu.bitcast(vt_ref[dr2, :], BF16), w_in, preferred_element_type=F32)
            rows = eb // nkeys // MM_PIECES
            for ii in range(q * rows, (q + 1) * rows):
                rs = slice(ii * nkeys, (ii + 1) * nkeys)
                rs2 = slice(ii * (nkeys // 2), (ii + 1) * (nkeys // 2))
                for lc in range(tm // LANES):
                    ls = slice(lc * LANES, (lc + 1) * LANES)
                    gate = zero
                    for h in range(heads):
                        cnt = cnt_ref[h, pl.ds(i0, SUBLANES), ls][ii:ii + 1]
                        c1 = c1_ref[h, pl.ds(i0, SUBLANES), ls][ii:ii + 1]
                        cnt = jnp.broadcast_to(cnt, (nkeys, LANES)).astype(BF16)
                        c1 = jnp.broadcast_to(c1, (nkeys, LANES)).astype(BF16)
                        r2 = pltpu.bitcast(r2_ref[h, :, ls], BF16)
                        e2 = pltpu.bitcast(e2_ref[h, :, ls], BF16)
                        gate = gate + jnp.where(r2 < cnt, e2, zero) * c1
                    act = _gelu_exact(ht_prev[rs, ls]).astype(BF16)
                    w_cur[rs2, ls] = pltpu.bitcast(gate * act, jnp.uint32)
            er = slice(q * (eb // MM_PIECES), (q + 1) * (eb // MM_PIECES))
            er2 = slice(q * (eb // 2 // MM_PIECES), (q + 1) * (eb // 2 // MM_PIECES))
            ht_cur[er, :] = jnp.dot(pltpu.bitcast(u_ref[er2, :], BF16), xn_t, preferred_element_type=F32)

    parity = lax.rem(s, 2)

    @pl.when(parity == 0)
    def _():
        step(ht0_ref, ht1_ref, w0_ref, w1_ref)

    @pl.when(parity == 1)
    def _():
        step(ht1_ref, ht0_ref, w1_ref, w0_ref)

    @pl.when(jnp.logical_and(s >= 2, eb_c == n_eb - 1))
    def _():
        y_ref[...] = _rmsnorm(h_ref[...] + acc_ref[...].T, g_ref[...])


def _peer_dense(xn, u, vt, r2, e2, cnt, c1, h2d, g, *, heads, nkeys, eb, tm):
    n, d = h2d.shape
    n_exp = 2 * u.shape[0]
    assert eb == SUBLANES * nkeys
    n_eb = n_exp // eb
    n_pairs = (n // tm) * n_eb
    kern = functools.partial(_peer_dense_kernel, heads=heads, nkeys=nkeys, eb=eb, tm=tm, n_eb=n_eb, n_pairs=n_pairs)

    def tb(offset):
        return lambda s: _pair_index(s, offset, n_pairs) // n_eb

    def ebk(offset):
        return lambda s: lax.rem(_pair_index(s, offset, n_pairs), n_eb)

    sel_spec = pl.BlockSpec((heads, nkeys, tm), lambda s: (0, 0, tb(1)(s)))
    pk_spec = pl.BlockSpec((heads, nkeys // 2, tm), lambda s: (0, 0, tb(1)(s)))
    return pl.pallas_call(
        kern,
        grid=(n_pairs + 2,),
        in_specs=[
            pl.BlockSpec((d // 2, tm), lambda s: (0, tb(0)(s))),
            pl.BlockSpec((eb // 2, d), lambda s: (ebk(0)(s), 0)),
            pl.BlockSpec((d // 2, eb), lambda s: (0, ebk(2)(s))),
            pk_spec, pk_spec, sel_spec, sel_spec,
            pl.BlockSpec((tm, d), lambda s: (tb(2)(s), 0)),
            pl.BlockSpec((1, d), lambda s: (0, 0)),
        ],
        out_specs=pl.BlockSpec((tm, d), lambda s: (tb(2)(s), 0)),
        out_shape=jax.ShapeDtypeStruct((n, d), F32),
        scratch_shapes=[
            pltpu.VMEM((d, tm), F32),
            pltpu.VMEM((eb, tm), F32),
            pltpu.VMEM((eb, tm), F32),
            pltpu.VMEM((eb // 2, tm), jnp.uint32),
            pltpu.VMEM((eb // 2, tm), jnp.uint32),
        ],
        compiler_params=_cparams(("arbitrary",)),
        name="peer_dense",
    )(xn, u, vt, r2, e2, cnt, c1, h2d, g)


def _pick_block(n, candidates):
    for c in candidates:
        if n % c == 0:
            return c
    raise ValueError(f"no block size in {candidates} divides {n}")


def _trunk(x, buf_a, buf_q, s0, wts, dims):
    heads, hd, cw, p_heads, nkeys, half = dims
    (g_mix, w_in, w_conv_a, w_conv_q, a_log_pad, dt_pad, g_dn, w_out_a, w_out_dn, w_o, g_ffn,
     wq_t, keys, u_pk, vt_pk, g_final) = wts
    bsz, tp, d = x.shape
    n = bsz * tp
    x2d = x.reshape(n, d)
    c = DN_CHUNK if tp % DN_CHUNK == 0 else SUBLANES
    ka, kq = w_conv_a.shape[0], w_conv_q.shape[0]
    last = tp - (pl.cdiv(tp, c) - 1) * c
    assert (tp % c == 0 or tp < c) and last >= kq - 1 and last >= ka - 1, (tp, c)

    proj = _in_proj(x2d, g_mix, w_in, _pick_block(n, (256, 128)))
    if c == DN_CHUNK:
        bb, unroll = _pick_block(bsz, (4, 2, 1)), 2
    else:
        bb, unroll = _pick_block(bsz, (16, 8, 4, 2, 1)), 16
    ya, og, nbuf_a, nbuf_q, s_new = _mixer(
        proj.reshape(bsz, tp, -1), buf_a, buf_q, s0, w_conv_a, w_conv_q, a_log_pad, dt_pad, g_dn,
        bb=bb, c=c, t_real=tp, heads=heads, hd=hd, unroll=min(unroll, bb))
    tm2 = _pick_block(n, (512, 256, 128))
    hres = _out_proj(x2d, ya.reshape(n, cw), og.reshape(n, cw), proj, w_out_a, w_out_dn, w_o, tm2,
                     gate_block=(6 * cw) // d)
    xn_t, r2, e2, cnt, c1 = _peer_select(hres, g_ffn, wq_t, keys, heads=p_heads, nkeys=nkeys, half=half,
                                         topk=PEER_TOPK, tms=LANES)
    tm3 = _pick_block(n, (512, 256, 128))
    y = _peer_dense(xn_t, u_pk, vt_pk, r2, e2, cnt, c1, hres, g_final, heads=p_heads, nkeys=nkeys,
                    eb=SUBLANES * nkeys, tm=tm3)
    return y.reshape(bsz, tp, d), nbuf_a, nbuf_q, s_new


def kernel(x_prompt, x_sample, state_conv_a, state_conv_qkv, state_delta, g_norm_mix, w_in, w_conv_a, w_conv_qkv,
           a_log, dt_bias, g_dn_norm, w_out_a, w_out_dn, w_o, g_norm_ffn, w_query, sub_keys, expert_u, expert_v,
           g_norm_final):
    depth = w_in.shape[0]
    assert depth == 1
    d = x_prompt.shape[-1]
    heads = a_log.shape[1]
    hd = g_dn_norm.shape[1]
    cw = heads * hd
    assert w_conv_a.shape[2] == cw and w_conv_qkv.shape[2] == 3 * cw and d == 2 * cw
    p_heads, nkeys, half = sub_keys.shape[2], sub_keys.shape[3], sub_keys.shape[4]
    assert nkeys == LANES and expert_u.shape[1] == nkeys * nkeys

    w = w_in[0]
    o_qkv, o_z, o_a, o_b, o_g = 3 * cw, 6 * cw, 7 * cw, 7 * cw + heads, 7 * cw + 2 * heads
    ab_pad = jnp.zeros((d, LANES - 2 * heads), w.dtype)
    w_r = jnp.concatenate([w[:, :o_z], w[:, o_g:], w[:, o_z:o_a], w[:, o_a:o_g], ab_pad], axis=1).astype(BF16)
    pad4 = lambda v: jnp.pad(v.astype(F32), ((0, 0), (0, LANES - v.shape[1])))
    wts = (
        g_norm_mix, w_r, w_conv_a[0], w_conv_qkv[0], pad4(a_log), pad4(dt_bias), g_dn_norm,
        w_out_a[0].astype(BF16), w_out_dn[0].astype(BF16), w_o[0].astype(BF16), g_norm_ffn,
        w_query[0].T.astype(BF16), sub_keys[0].reshape(2 * p_heads, nkeys, half).astype(BF16),
        *_pack_experts(expert_u[0], expert_v[0], SUBLANES * nkeys), g_norm_final.reshape(1, d),
    )
    dims = (heads, hd, cw, p_heads, nkeys, half)

    bp, tpr, _ = x_prompt.shape
    assert tpr % DN_CHUNK == 0
    ka, kq = w_conv_a.shape[1], w_conv_qkv.shape[1]
    zero_a = jnp.zeros((bp, ka - 1, cw), F32)
    zero_q = jnp.zeros((bp, kq - 1, 3 * cw), F32)
    zero_s = jnp.zeros((bp, heads, hd, hd), F32)
    y_p, a_p, q_p, s_p = _trunk(x_prompt, zero_a, zero_q, zero_s, wts, dims)

    assert x_sample.shape[1] <= SUBLANES
    y_s, a_s, q_s, s_s = _trunk(x_sample, state_conv_a[0], state_conv_qkv[0], state_delta[0], wts, dims)
    return (y_p, y_s, a_p[None], q_p[None], s_p[None], a_s[None], q_s[None], s_s[None])
```

```python
import functools

import jax
import jax.numpy as jnp
from jax import lax
from jax.experimental import pallas as pl
from jax.experimental.pallas import tpu as pltpu

EPS = 1e-6
F32 = jnp.float32
BF16 = jnp.bfloat16
LANES = 128
SUBLANES = 8
VMEM_LIMIT = 56 * 1024 * 1024

PEER_TOPK = 16
DN_CHUNK = 128
MM_PIECES = 8
NEG_INF = float("-inf")


def _cparams(sem):
    return pltpu.CompilerParams(dimension_semantics=sem, vmem_limit_bytes=VMEM_LIMIT)


def _mm(a, b):
    return jnp.dot(a.astype(BF16), b.astype(BF16), preferred_element_type=F32)


def _mm_nt(a, b):
    return lax.dot_general(a.astype(BF16), b.astype(BF16), (((1,), (1,)), ((), ())), preferred_element_type=F32)


def _mm_tn(a, b):
    return lax.dot_general(a.astype(BF16), b.astype(BF16), (((0,), (0,)), ((), ())), preferred_element_type=F32)


def _rmsnorm(x, g):
    return x * lax.rsqrt(jnp.mean(x * x, axis=-1, keepdims=True) + EPS) * g


def _sigmoid(x):
    return 1.0 / (1.0 + jnp.exp(-x))


def _silu(x):
    return x * _sigmoid(x)


def _softplus(x):
    return jnp.maximum(x, 0.0) + jnp.log(1.0 + jnp.exp(-jnp.abs(x)))


def _gelu_exact(x):
    return 0.5 * x * (1.0 + lax.erf(x * (2.0 ** -0.5)))


def _in_proj_kernel(x_ref, g_ref, w_ref, o_ref):
    xn = _rmsnorm(x_ref[...], g_ref[...]).astype(BF16)
    o_ref[...] = jnp.dot(xn, w_ref[...], preferred_element_type=F32)


def _in_proj(x2d, g, w, tm):
    n, d = x2d.shape
    width = w.shape[1]
    return pl.pallas_call(
        _in_proj_kernel,
        grid=(n // tm,),
        in_specs=[
            pl.BlockSpec((tm, d), lambda i: (i, 0)),
            pl.BlockSpec((1, d), lambda i: (0, 0)),
            pl.BlockSpec((d, width), lambda i: (0, 0)),
        ],
        out_specs=pl.BlockSpec((tm, width), lambda i: (i, 0)),
        out_shape=jax.ShapeDtypeStruct((n, width), F32),
        compiler_params=_cparams(("parallel",)),
        name="in_proj",
    )(x2d, g, w)


def _shift_rows(xp, k, rows):
    if k == 0:
        return xp[SUBLANES:SUBLANES + rows]
    return pltpu.roll(xp, k, axis=0)[SUBLANES:SUBLANES + rows]


def _cumsum_rows(x, rows):
    ridx = lax.broadcasted_iota(jnp.int32, x.shape, 0)
    s = 1
    while s < rows:
        x = x + jnp.where(ridx >= s, pltpu.roll(x, s, axis=0), 0.0)
        s *= 2
    return x


def _unit_lower_inverse_many(ms, c):
    eye = (lax.broadcasted_iota(jnp.int32, (c, c), 0) == lax.broadcasted_iota(jnp.int32, (c, c), 1)).astype(F32)
    n_factors = c.bit_length() - 1
    assert 1 << n_factors == c
    ps = [-m for m in ms]
    ts = [eye + p for p in ps]
    if n_factors == 1:
        return ts
    ps = [_mm(p, p) for p in ps]
    for _ in range(n_factors - 2):
        tps = [_mm(jnp.concatenate([t, p], axis=0), p) for t, p in zip(ts, ps)]
        ts = [t + tp[:c] for t, tp in zip(ts, tps)]
        ps = [tp[c:] for tp in tps]
    return [t + _mm(t, p) for t, p in zip(ts, ps)]


def _mixer_kernel(bch_ref, qkv_ref, z_ref, ab_ref, bufa_ref, bufq_ref, s0_ref,
                  wca_ref, wcq_ref, alog_ref, dtb_ref, gdn_ref,
                  ya_ref, og_ref, nbufa_ref, nbufq_ref, sout_ref,
                  halo_a, halo_q, state,
                  *, bb, c, heads, hd, t_real, n_chunks, unroll):
    ci = pl.program_id(1)
    cw = heads * hd
    ka = wca_ref.shape[0]
    kq = wcq_ref.shape[0]

    @pl.when(ci == 0)
    def _():
        state[...] = s0_ref[...]
        halo_a[...] = jnp.zeros(halo_a.shape, F32)
        halo_q[...] = jnp.zeros(halo_q.shape, F32)
        halo_a[:, SUBLANES - (ka - 1):, :] = bufa_ref[...]
        halo_q[:, SUBLANES - (kq - 1):, :] = bufq_ref[...]

    wca = wca_ref[...]
    wcq = wcq_ref[...]
    neg_a = -jnp.exp(alog_ref[...])
    dtb = dtb_ref[...]
    gdn = gdn_ref[...]
    rows_left = t_real - ci * c
    ridx = lax.broadcasted_iota(jnp.int32, (c, 1), 0)
    valid = ridx < rows_left
    ri = lax.broadcasted_iota(jnp.int32, (c, c), 0)
    si = lax.broadcasted_iota(jnp.int32, (c, c), 1)
    causal = ri >= si
    strict = ri > si
    cpad = max(c, LANES)

    rb = bch_ref.shape[1]

    def pad_rows(x):
        if rb == c:
            return x
        return jnp.concatenate([x, jnp.zeros((c - rb, x.shape[1]), x.dtype)], axis=0)

    def load_seq(b):
        return (pad_rows(bch_ref[b]), pad_rows(qkv_ref[b]), pad_rows(z_ref[b]), pad_rows(ab_ref[b]),
                halo_a[b], halo_q[b], [state[b, h] for h in range(heads)])

    def prep_seq(bch, qkv, z, ab, ha, hq, s_prev_all):
        b_c, c_c, h_c = bch[:, :cw], bch[:, cw:2 * cw], bch[:, 2 * cw:]
        ta = c_c * h_c
        xpa = jnp.concatenate([ha, ta], axis=0)
        conv_a = wca[ka - 1:ka] * _shift_rows(xpa, 0, c)
        for j in range(1, ka):
            conv_a = conv_a + wca[ka - 1 - j:ka - j] * _shift_rows(xpa, j, c)
        ya = b_c * conv_a
        xpq = jnp.concatenate([hq, qkv], axis=0)
        conv_q = wcq[kq - 1:kq] * _shift_rows(xpq, 0, c)
        for j in range(1, kq):
            conv_q = conv_q + wcq[kq - 1 - j:kq - j] * _shift_rows(xpq, j, c)
        conv_q = _silu(conv_q)
        new_ha, new_hq = xpa[c:c + SUBLANES], xpq[c:c + SUBLANES]
        g_all = jnp.where(valid, neg_a * _softplus(ab + dtb), 0.0)
        beta_all = jnp.where(valid, _sigmoid(ab), 0.0)
        gcum = _cumsum_rows(g_all, c)
        if c < cpad:
            gpad = jnp.concatenate([gcum, jnp.zeros((cpad - c, LANES), F32)], axis=0)
        else:
            gpad = gcum
        gcum_t = gpad.T
        chains = []
        for h in range(heads):
            q = conv_q[:, h * hd:(h + 1) * hd]
            k = conv_q[:, cw + h * hd:cw + (h + 1) * hd]
            v = conv_q[:, 2 * cw + h * hd:2 * cw + (h + 1) * hd]
            q = q * lax.rsqrt(jnp.sum(q * q, axis=-1, keepdims=True) + EPS) * (hd ** -0.5)
            k = k * lax.rsqrt(jnp.sum(k * k, axis=-1, keepdims=True) + EPS)
            q = jnp.where(valid, q, 0.0)
            k = jnp.where(valid, k, 0.0)
            v = jnp.where(valid, v, 0.0)
            gc = gcum[:, h:h + 1]
            gr = gcum_t[h:h + 1, :c]
            beta = beta_all[:, heads + h:heads + h + 1]
            g_last = gcum[c - 1:c, h:h + 1]
            eg = jnp.exp(gc)
            chains.append(dict(
                q=q, k=k, beta=beta, g_last=g_last, s_prev=s_prev_all[h], z=z[:, h * hd:(h + 1) * hd],
                decay=jnp.exp(jnp.where(causal, gc - gr, NEG_INF)),
                rhs=jnp.concatenate([v * beta, k * (beta * eg)], axis=-1),
                q_dec=q * eg, k_dec=k * jnp.exp(g_last - gc)))
        return ya, new_ha, new_hq, chains

    def delta_chains(ch):
        kkqk = [_mm_nt(jnp.concatenate([x["k"], x["q"]], axis=0), x["k"]) for x in ch]
        ms = [jnp.where(strict, kq_[:c] * x["decay"] * x["beta"], 0.0) for kq_, x in zip(kkqk, ch)]
        qks = [kq_[c:] * x["decay"] for kq_, x in zip(kkqk, ch)]
        t_invs = _unit_lower_inverse_many(ms, c)
        sols = [_mm(t, x["rhs"]) for t, x in zip(t_invs, ch)]
        ws_qs = [_mm(jnp.concatenate([sol[:, hd:], x["q_dec"]], axis=0), x["s_prev"])
                 for sol, x in zip(sols, ch)]
        v_news = [sol[:, :hd] - wq[:c] for sol, wq in zip(sols, ws_qs)]
        if c >= LANES:
            rs = [_mm(jnp.concatenate([qk, x["k_dec"].T], axis=0), vn)
                  for qk, x, vn in zip(qks, ch, v_news)]
            os_ = [wq[c:] + r[:c] for wq, r in zip(ws_qs, rs)]
            upd = [r[c:] for r in rs]
        else:
            os_ = [wq[c:] + _mm(qk, vn) for wq, qk, vn in zip(ws_qs, qks, v_news)]
            upd = [_mm_tn(x["k_dec"], vn) for x, vn in zip(ch, v_news)]
        states = [x["s_prev"] * jnp.exp(x["g_last"]) + u_ for x, u_ in zip(ch, upd)]
        outs = [o * lax.rsqrt(jnp.mean(o * o, axis=-1, keepdims=True) + EPS) * gdn * _silu(x["z"])
                for o, x in zip(os_, ch)]
        return outs, states

    def store_seq(b, ya, og, new_ha, new_hq, new_states):
        ya_ref[b] = ya[:rb]
        og_ref[b] = og[:rb]
        halo_a[b] = new_ha
        halo_q[b] = new_hq
        for h in range(heads):
            state[b, h] = new_states[h]

    def per_group(gi, carry):
        b0 = gi * unroll
        preps = [prep_seq(*load_seq(b0 + j)) for j in range(unroll)]
        outs, states = delta_chains([x for pr in preps for x in pr[3]])
        for j, (ya, new_ha, new_hq, _) in enumerate(preps):
            og = jnp.concatenate(outs[j * heads:(j + 1) * heads], axis=-1)
            store_seq(b0 + j, ya, og, new_ha, new_hq, states[j * heads:(j + 1) * heads])
        return carry

    lax.fori_loop(0, bb // unroll, per_group, 0)

    @pl.when(ci == n_chunks - 1)
    def _():
        sout_ref[...] = state[...]
        last = t_real - (n_chunks - 1) * c
        nbufa_ref[...] = (bch_ref[:, last - (ka - 1):last, cw:2 * cw]
                          * bch_ref[:, last - (ka - 1):last, 2 * cw:])
        nbufq_ref[...] = qkv_ref[:, last - (kq - 1):last, :]


def _mixer(proj3, buf_a, buf_q, s0, w_conv_a, w_conv_q, a_log_pad, dt_pad, g_dn, *, bb, c, t_real, heads, hd,
           unroll):
    bsz, tp, _ = proj3.shape
    cw = heads * hd
    n_chunks = pl.cdiv(tp, c)
    rb = min(c, tp)
    assert n_chunks * rb == tp
    ka, kq = w_conv_a.shape[0], w_conv_q.shape[0]
    kern = functools.partial(_mixer_kernel, bb=bb, c=c, heads=heads, hd=hd, t_real=t_real, n_chunks=n_chunks,
                             unroll=unroll)
    return pl.pallas_call(
        kern,
        grid=(bsz // bb, n_chunks),
        in_specs=[
            pl.BlockSpec((bb, rb, 3 * cw), lambda i, j: (i, j, 0)),
            pl.BlockSpec((bb, rb, 3 * cw), lambda i, j: (i, j, 1)),
            pl.BlockSpec((bb, rb, cw), lambda i, j: (i, j, (3 * cw + 3 * cw + 4 * cw) // cw)),
            pl.BlockSpec((bb, rb, LANES), lambda i, j: (i, j, (3 * cw + 3 * cw + 4 * cw + cw) // LANES)),
            pl.BlockSpec((bb, ka - 1, cw), lambda i, j: (i, 0, 0)),
            pl.BlockSpec((bb, kq - 1, 3 * cw), lambda i, j: (i, 0, 0)),
            pl.BlockSpec((bb, heads, hd, hd), lambda i, j: (i, 0, 0, 0)),
            pl.BlockSpec((ka, cw), lambda i, j: (0, 0)),
            pl.BlockSpec((kq, 3 * cw), lambda i, j: (0, 0)),
            pl.BlockSpec((1, LANES), lambda i, j: (0, 0)),
            pl.BlockSpec((1, LANES), lambda i, j: (0, 0)),
            pl.BlockSpec((1, hd), lambda i, j: (0, 0)),
        ],
        out_specs=[
            pl.BlockSpec((bb, rb, cw), lambda i, j: (i, j, 0)),
            pl.BlockSpec((bb, rb, cw), lambda i, j: (i, j, 0)),
            pl.BlockSpec((bb, ka - 1, cw), lambda i, j: (i, 0, 0)),
            pl.BlockSpec((bb, kq - 1, 3 * cw), lambda i, j: (i, 0, 0)),
            pl.BlockSpec((bb, heads, hd, hd), lambda i, j: (i, 0, 0, 0)),
        ],
        out_shape=[
            jax.ShapeDtypeStruct((bsz, tp, cw), F32),
            jax.ShapeDtypeStruct((bsz, tp, cw), F32),
            jax.ShapeDtypeStruct((bsz, ka - 1, cw), F32),
            jax.ShapeDtypeStruct((bsz, kq - 1, 3 * cw), F32),
            jax.ShapeDtypeStruct((bsz, heads, hd, hd), F32),
        ],
        scratch_shapes=[
            pltpu.VMEM((bb, SUBLANES, cw), F32),
            pltpu.VMEM((bb, SUBLANES, 3 * cw), F32),
            pltpu.VMEM((bb, heads, hd, hd), F32),
        ],
        compiler_params=_cparams(("parallel", "arbitrary")),
        name="mixer",
    )(proj3, proj3, proj3, proj3, buf_a, buf_q, s0, w_conv_a, w_conv_q, a_log_pad, dt_pad, g_dn)


def _out_proj_kernel(x_ref, ya_ref, og_ref, ga_ref, gb_ref, woa_ref, wod_ref, wo_ref, h_ref):
    y_a = jnp.dot(ya_ref[...].astype(BF16), woa_ref[...], preferred_element_type=F32)
    y_b = jnp.dot(og_ref[...].astype(BF16), wod_ref[...], preferred_element_type=F32)
    m = _sigmoid(ga_ref[...]) * y_a + _sigmoid(gb_ref[...]) * y_b
    h_ref[...] = x_ref[...] + jnp.dot(m.astype(BF16), wo_ref[...], preferred_element_type=F32)


def _out_proj(x2d, ya, og, proj, w_out_a, w_out_dn, w_o, tm, gate_block):
    n, d = x2d.shape
    cw = ya.shape[1]
    return pl.pallas_call(
        _out_proj_kernel,
        grid=(n // tm,),
        in_specs=[
            pl.BlockSpec((tm, d), lambda i: (i, 0)),
            pl.BlockSpec((tm, cw), lambda i: (i, 0)),
            pl.BlockSpec((tm, cw), lambda i: (i, 0)),
            pl.BlockSpec((tm, d), lambda i: (i, gate_block)),
            pl.BlockSpec((tm, d), lambda i: (i, gate_block + 1)),
            pl.BlockSpec((cw, d), lambda i: (0, 0)),
            pl.BlockSpec((cw, d), lambda i: (0, 0)),
            pl.BlockSpec((d, d), lambda i: (0, 0)),
        ],
        out_specs=pl.BlockSpec((tm, d), lambda i: (i, 0)),
        out_shape=jax.ShapeDtypeStruct((n, d), F32),
        compiler_params=_cparams(("parallel",)),
        name="out_proj",
    )(x2d, ya, og, proj, proj, w_out_a, w_out_dn, w_o)


def _pack_u_kernel(u_ref, o_ref):
    o_ref[...] = pltpu.bitcast(u_ref[...].astype(BF16), jnp.uint32)


def _pack_vt_kernel(v_ref, o_ref):
    o_ref[...] = pltpu.bitcast(v_ref[...].T.astype(BF16), jnp.uint32)


def _pack_experts(expert_u, expert_v, rows):
    n_exp, d = expert_u.shape
    u_pk = pl.pallas_call(
        _pack_u_kernel,
        grid=(n_exp // rows,),
        in_specs=[pl.BlockSpec((rows, d), lambda i: (i, 0))],
        out_specs=pl.BlockSpec((rows // 2, d), lambda i: (i, 0)),
        out_shape=jax.ShapeDtypeStruct((n_exp // 2, d), jnp.uint32),
        compiler_params=_cparams(("parallel",)),
        name="pack_u",
    )(expert_u)
    vt_pk = pl.pallas_call(
        _pack_vt_kernel,
        grid=(n_exp // rows,),
        in_specs=[pl.BlockSpec((rows, d), lambda i: (i, 0))],
        out_specs=pl.BlockSpec((d // 2, rows), lambda i: (0, i)),
        out_shape=jax.ShapeDtypeStruct((d // 2, n_exp), jnp.uint32),
        compiler_params=_cparams(("parallel",)),
        name="pack_vt",
    )(expert_v)
    return u_pk, vt_pk


def _extract_topk(ss, flat, topk, exact_ties):
    n, t = ss[0].shape
    big = jnp.int32(2 ** 30)
    ro = lax.broadcasted_iota(jnp.int32, (topk, t), 0)
    cur = list(ss)
    ranks = [jnp.full((n, t), float(topk), F32) for _ in ss]
    vals = [jnp.zeros((topk, t), F32) for _ in ss]
    for r in range(topk):
        for i in range(len(ss)):
            m = jnp.max(cur[i], axis=0, keepdims=True)
            hit = cur[i] == m
            if exact_ties:
                hit = flat == jnp.min(jnp.where(hit, flat, big), axis=0, keepdims=True)
            ranks[i] = jnp.where(hit, float(r), ranks[i])
            cur[i] = jnp.where(hit, NEG_INF, cur[i])
            vals[i] = jnp.where(ro == r, m, vals[i])
    return vals, ranks


def _tie_count(ranks, topk):
    bad = None
    for rk in ranks:
        n_marked = jnp.sum(jnp.where(rk < topk, 1.0, 0.0), axis=0, keepdims=True)
        bad = jnp.abs(n_marked - topk) if bad is None else bad + jnp.abs(n_marked - topk)
    return bad


def _pair_candidates(v1, v2, topk):
    io8 = lax.broadcasted_iota(jnp.int32, (SUBLANES, 1), 0)
    pieces = [v1[0:1] + v2]
    flats = [lax.broadcasted_iota(jnp.int32, (topk, 1), 0)]
    for a in range(1, SUBLANES):
        pieces.append(jnp.where(io8 < (topk // (a + 1)), v1[a:a + 1] + v2[0:SUBLANES], NEG_INF))
        flats.append(a * topk + io8)
    pieces.append(v1[SUBLANES:] + v2[0:1])
    flats.append((SUBLANES + io8) * topk)
    return jnp.concatenate(pieces, axis=0), jnp.concatenate(flats, axis=0)


def _pair_counts(sums, rank, mtot, topk):
    t = sums.shape[1]
    sel = jnp.where(rank < topk, 1.0, 0.0)
    z = jnp.sum(sel * jnp.exp(jnp.where(rank < topk, sums, mtot) - mtot), axis=0, keepdims=True)
    ro = lax.broadcasted_iota(jnp.int32, (topk, t), 0)
    nb = jnp.zeros((topk, t), F32)
    nb = jnp.where(ro == 0, jnp.sum(sel[0:topk], axis=0, keepdims=True), nb)
    for a in range(1, SUBLANES):
        lo = topk + (a - 1) * SUBLANES
        nb = jnp.where(ro == a, jnp.sum(sel[lo:lo + SUBLANES], axis=0, keepdims=True), nb)
    tail = sel[topk + (SUBLANES - 1) * SUBLANES:]
    nb = jnp.where(ro >= SUBLANES, jnp.concatenate([tail, tail], axis=0), nb)
    return nb, z


def _peer_select_kernel(h_ref, g_ref, wqt_ref, keys_ref, xn_ref, r2_ref, e2_ref, cnt_ref, c1_ref, q_scr,
                        *, heads, half, topk, group):
    xf = _rmsnorm(h_ref[...], g_ref[...])
    xn = xf.astype(BF16)
    xn_ref[...] = pltpu.bitcast(xf.T.astype(BF16), jnp.uint32)
    q_scr[...] = lax.dot_general(wqt_ref[...], xn, (((1,), (1,)), ((), ())), preferred_element_type=F32)
    nkeys = keys_ref.shape[1]
    key_idx = lax.broadcasted_iota(jnp.int32, (nkeys, 1), 0)

    def select_heads(exact_ties):
        def per_group(gi, bad):
            hs = [gi * group + j for j in range(group)]
            scores = []
            for h in hs:
                base = pl.multiple_of(h * 2 * half, 2 * half)
                scores.append(_mm(keys_ref[h], q_scr[pl.ds(base, half), :]))
                scores.append(_mm(keys_ref[heads + h], q_scr[pl.ds(base + half, half), :]))
            vals, ranks = _extract_topk(scores, key_idx, topk, exact_ties)
            cands = [_pair_candidates(vals[2 * j], vals[2 * j + 1], topk) for j in range(group)]
            _, pair_ranks = _extract_topk([cd[0] for cd in cands], cands[0][1], topk, exact_ties)
            for j, h in enumerate(hs):
                v1, r1, v2, r2 = vals[2 * j], ranks[2 * j], vals[2 * j + 1], ranks[2 * j + 1]
                nb, z = _pair_counts(cands[j][0], pair_ranks[j], v1[0:1] + v2[0:1], topk)
                cnt = jnp.zeros(r1.shape, F32)
                for a in range(topk):
                    cnt = jnp.where(r1 == float(a), nb[a:a + 1], cnt)
                r2_ref[h, 0] = pltpu.bitcast(r2.astype(BF16), jnp.uint32)
                e2_ref[h, 0] = pltpu.bitcast(jnp.exp(scores[2 * j + 1] - v2[0:1]).astype(BF16), jnp.uint32)
                cnt_ref[h, 0] = cnt
                c1_ref[h, 0] = jnp.exp(scores[2 * j] - v1[0:1]) / z
            return bad + _tie_count(ranks + pair_ranks, topk)

        return lax.fori_loop(0, heads // group, per_group, jnp.zeros((1, h_ref.shape[0]), F32))

    ties = select_heads(exact_ties=False)

    @pl.when(jnp.max(ties) > 0.0)
    def _():
        select_heads(exact_ties=True)


def _peer_select(h2d, g, wq_t, keys, *, heads, nkeys, half, topk, tms):
    n, d = h2d.shape
    qd = wq_t.shape[0]
    kern = functools.partial(_peer_select_kernel, heads=heads, half=half, topk=topk, group=2)
    assert tms == LANES
    sel_f32 = jax.ShapeDtypeStruct((heads, n // LANES, nkeys, LANES), F32)
    sel_pk = jax.ShapeDtypeStruct((heads, n // LANES, nkeys // 2, LANES), jnp.uint32)
    sel_spec = pl.BlockSpec((heads, 1, nkeys, LANES), lambda i: (0, i, 0, 0))
    pk_spec = pl.BlockSpec((heads, 1, nkeys // 2, LANES), lambda i: (0, i, 0, 0))
    return pl.pallas_call(
        kern,
        grid=(n // tms,),
        in_specs=[
            pl.BlockSpec((tms, d), lambda i: (i, 0)),
            pl.BlockSpec((1, d), lambda i: (0, 0)),
            pl.BlockSpec((qd, d), lambda i: (0, 0)),
            pl.BlockSpec(keys.shape, lambda i: (0, 0, 0)),
        ],
        out_specs=[pl.BlockSpec((d // 2, tms), lambda i: (0, i)), pk_spec, pk_spec, sel_spec, sel_spec],
        out_shape=[jax.ShapeDtypeStruct((d // 2, n), jnp.uint32), sel_pk, sel_pk, sel_f32, sel_f32],
        scratch_shapes=[pltpu.VMEM((qd, tms), F32)],
        compiler_params=_cparams(("parallel",)),
        name="peer_select",
    )(h2d, g, wq_t, keys)


def _pair_index(s, offset, n_pairs):
    return jnp.clip(s - offset, 0, n_pairs - 1)


def _peer_dense_kernel(xn_ref, u_ref, vt_ref, r2_ref, e2_ref, cnt_ref, c1_ref, h_ref, g_ref, y_ref,
                       acc_ref, ht0_ref, ht1_ref, w0_ref, w1_ref, *, heads, nkeys, eb, tm, n_eb, n_pairs):
    s = pl.program_id(0)
    eb_b = lax.rem(_pair_index(s, 1, n_pairs), n_eb)
    eb_c = lax.rem(_pair_index(s, 2, n_pairs), n_eb)

    @pl.when(s == 0)
    def _():
        ht1_ref[...] = jnp.zeros(ht1_ref.shape, F32)
        w0_ref[...] = jnp.zeros(w0_ref.shape, jnp.uint32)
        w1_ref[...] = jnp.zeros(w1_ref.shape, jnp.uint32)

    @pl.when(eb_c == 0)
    def _():
        acc_ref[...] = jnp.zeros(acc_ref.shape, F32)

    i0 = pl.multiple_of(eb_b * (eb // nkeys), SUBLANES)
    zero = jnp.zeros((nkeys, LANES), BF16)

    def step(ht_cur, ht_prev, w_cur, w_prev):
        d = acc_ref.shape[0]
        xn_t = pltpu.bitcast(xn_ref[...], BF16)
        w_in = jnp.concatenate([pltpu.bitcast(w_prev[lc], BF16) for lc in range(tm // LANES)], axis=1)
        for q in range(MM_PIECES):
            dr = slice(q * (d // MM_PIECES), (q + 1) * (d // MM_PIECES))
            dr2 = slice(q * (d // 2 // MM_PIECES), (q + 1) * (d // 2 // MM_PIECES))
            acc_ref[dr, :] += jnp.dot(pltpu.bitcast(vt_ref[dr2, :], BF16), w_in, preferred_element_type=F32)
            rows = eb // nkeys // MM_PIECES
            for ii in range(q * rows, (q + 1) * rows):
                rs = slice(ii * nkeys, (ii + 1) * nkeys)
                rs2 = slice(ii * (nkeys // 2), (ii + 1) * (nkeys // 2))
                for lc in range(tm // LANES):
                    gate = zero
                    for h in range(heads):
                        cnt = cnt_ref[h, lc, pl.ds(i0, SUBLANES), :][ii:ii + 1]
                        c1 = c1_ref[h, lc, pl.ds(i0, SUBLANES), :][ii:ii + 1]
                        cnt = jnp.broadcast_to(cnt, (nkeys, LANES)).astype(BF16)
                        c1 = jnp.broadcast_to(c1, (nkeys, LANES)).astype(BF16)
                        r2 = pltpu.bitcast(r2_ref[h, lc], BF16)
                        e2 = pltpu.bitcast(e2_ref[h, lc], BF16)
                        gate = gate + jnp.where(r2 < cnt, e2, zero) * c1
                    act = _gelu_exact(ht_prev[lc, rs, :]).astype(BF16)
                    w_cur[lc, rs2, :] = pltpu.bitcast(gate * act, jnp.uint32)
            er = slice(q * (eb // MM_PIECES), (q + 1) * (eb // MM_PIECES))
            er2 = slice(q * (eb // 2 // MM_PIECES), (q + 1) * (eb // 2 // MM_PIECES))
            res = jnp.dot(pltpu.bitcast(u_ref[er2, :], BF16), xn_t, preferred_element_type=F32)
            for lc in range(tm // LANES):
                ht_cur[lc, er, :] = res[:, lc * LANES:(lc + 1) * LANES]

    parity = lax.rem(s, 2)

    @pl.when(parity == 0)
    def _():
        step(ht0_ref, ht1_ref, w0_ref, w1_ref)

    @pl.when(parity == 1)
    def _():
        step(ht1_ref, ht0_ref, w1_ref, w0_ref)

    @pl.when(jnp.logical_and(s >= 2, eb_c == n_eb - 1))
    def _():
        y_ref[...] = _rmsnorm(h_ref[...] + acc_ref[...].T, g_ref[...])


def _peer_dense(xn, u, vt, r2, e2, cnt, c1, h2d, g, *, heads, nkeys, eb, tm):
    n, d = h2d.shape
    n_exp = 2 * u.shape[0]
    assert eb == SUBLANES * nkeys
    n_eb = n_exp // eb
    n_pairs = (n // tm) * n_eb
    kern = functools.partial(_peer_dense_kernel, heads=heads, nkeys=nkeys, eb=eb, tm=tm, n_eb=n_eb, n_pairs=n_pairs)

    def tb(offset):
        return lambda s: _pair_index(s, offset, n_pairs) // n_eb

    def ebk(offset):
        return lambda s: lax.rem(_pair_index(s, offset, n_pairs), n_eb)

    sel_spec = pl.BlockSpec((heads, tm // LANES, nkeys, LANES), lambda s: (0, tb(1)(s), 0, 0))
    pk_spec = pl.BlockSpec((heads, tm // LANES, nkeys // 2, LANES), lambda s: (0, tb(1)(s), 0, 0))
    return pl.pallas_call(
        kern,
        grid=(n_pairs + 2,),
        in_specs=[
            pl.BlockSpec((d // 2, tm), lambda s: (0, tb(0)(s))),
            pl.BlockSpec((eb // 2, d), lambda s: (ebk(0)(s), 0)),
            pl.BlockSpec((d // 2, eb), lambda s: (0, ebk(2)(s))),
            pk_spec, pk_spec, sel_spec, sel_spec,
            pl.BlockSpec((tm, d), lambda s: (tb(2)(s), 0)),
            pl.BlockSpec((1, d), lambda s: (0, 0)),
        ],
        out_specs=pl.BlockSpec((tm, d), lambda s: (tb(2)(s), 0)),
        out_shape=jax.ShapeDtypeStruct((n, d), F32),
        scratch_shapes=[
            pltpu.VMEM((d, tm), F32),
            pltpu.VMEM((tm // LANES, eb, LANES), F32),
            pltpu.VMEM((tm // LANES, eb, LANES), F32),
            pltpu.VMEM((tm // LANES, eb // 2, LANES), jnp.uint32),
            pltpu.VMEM((tm // LANES, eb // 2, LANES), jnp.uint32),
        ],
        compiler_params=_cparams(("arbitrary",)),
        name="peer_dense",
    )(xn, u, vt, r2, e2, cnt, c1, h2d, g)


def _pick_block(n, candidates):
    for c in candidates:
        if n % c == 0:
            return c
    raise ValueError(f"no block size in {candidates} divides {n}")


def _trunk(x, buf_a, buf_q, s0, wts, dims):
    heads, hd, cw, p_heads, nkeys, half = dims
    (g_mix, w_in, w_conv_a, w_conv_q, a_log_pad, dt_pad, g_dn, w_out_a, w_out_dn, w_o, g_ffn,
     wq_t, keys, u_pk, vt_pk, g_final) = wts
    bsz, tp, d = x.shape
    n = bsz * tp
    x2d = x.reshape(n, d)
    c = DN_CHUNK if tp % DN_CHUNK == 0 else SUBLANES
    ka, kq = w_conv_a.shape[0], w_conv_q.shape[0]
    last = tp - (pl.cdiv(tp, c) - 1) * c
    assert (tp % c == 0 or tp < c) and last >= kq - 1 and last >= ka - 1, (tp, c)

    proj = _in_proj(x2d, g_mix, w_in, _pick_block(n, (256, 128)))
    if c == DN_CHUNK:
        bb, unroll = _pick_block(bsz, (4, 2, 1)), 2
    else:
        bb, unroll = _pick_block(bsz, (16, 8, 4, 2, 1)), 16
    ya, og, nbuf_a, nbuf_q, s_new = _mixer(
        proj.reshape(bsz, tp, -1), buf_a, buf_q, s0, w_conv_a, w_conv_q, a_log_pad, dt_pad, g_dn,
        bb=bb, c=c, t_real=tp, heads=heads, hd=hd, unroll=min(unroll, bb))
    tm2 = _pick_block(n, (512, 256, 128))
    hres = _out_proj(x2d, ya.reshape(n, cw), og.reshape(n, cw), proj, w_out_a, w_out_dn, w_o, tm2,
                     gate_block=(6 * cw) // d)
    xn_t, r2, e2, cnt, c1 = _peer_select(hres, g_ffn, wq_t, keys, heads=p_heads, nkeys=nkeys, half=half,
                                         topk=PEER_TOPK, tms=LANES)
    tm3 = _pick_block(n, (512, 256, 128))
    y = _peer_dense(xn_t, u_pk, vt_pk, r2, e2, cnt, c1, hres, g_final, heads=p_heads, nkeys=nkeys,
                    eb=SUBLANES * nkeys, tm=tm3)
    return y.reshape(bsz, tp, d), nbuf_a, nbuf_q, s_new


def kernel(x_prompt, x_sample, state_conv_a, state_conv_qkv, state_delta, g_norm_mix, w_in, w_conv_a, w_conv_qkv,
           a_log, dt_bias, g_dn_norm, w_out_a, w_out_dn, w_o, g_norm_ffn, w_query, sub_keys, expert_u, expert_v,
           g_norm_final):
    depth = w_in.shape[0]
    assert depth == 1
    d = x_prompt.shape[-1]
    heads = a_log.shape[1]
    hd = g_dn_norm.shape[1]
    cw = heads * hd
    assert w_conv_a.shape[2] == cw and w_conv_qkv.shape[2] == 3 * cw and d == 2 * cw
    p_heads, nkeys, half = sub_keys.shape[2], sub_keys.shape[3], sub_keys.shape[4]
    assert nkeys == LANES and expert_u.shape[1] == nkeys * nkeys

    w = w_in[0]
    o_qkv, o_z, o_a, o_b, o_g = 3 * cw, 6 * cw, 7 * cw, 7 * cw + heads, 7 * cw + 2 * heads
    ab_pad = jnp.zeros((d, LANES - 2 * heads), w.dtype)
    w_r = jnp.concatenate([w[:, :o_z], w[:, o_g:], w[:, o_z:o_a], w[:, o_a:o_g], ab_pad], axis=1).astype(BF16)
    pad4 = lambda v: jnp.pad(v.astype(F32), ((0, 0), (0, LANES - v.shape[1])))
    wts = (
        g_norm_mix, w_r, w_conv_a[0], w_conv_qkv[0], pad4(a_log), pad4(dt_bias), g_dn_norm,
        w_out_a[0].astype(BF16), w_out_dn[0].astype(BF16), w_o[0].astype(BF16), g_norm_ffn,
        w_query[0].T.astype(BF16), sub_keys[0].reshape(2 * p_heads, nkeys, half).astype(BF16),
        *_pack_experts(expert_u[0], expert_v[0], SUBLANES * nkeys), g_norm_final.reshape(1, d),
    )
    dims = (heads, hd, cw, p_heads, nkeys, half)

    bp, tpr, _ = x_prompt.shape
    assert tpr % DN_CHUNK == 0
    ka, kq = w_conv_a.shape[1], w_conv_qkv.shape[1]
    zero_a = jnp.zeros((bp, ka - 1, cw), F32)
    zero_q = jnp.zeros((bp, kq - 1, 3 * cw), F32)
    zero_s = jnp.zeros((bp, heads, hd, hd), F32)
    y_p, a_p, q_p, s_p = _trunk(x_prompt, zero_a, zero_q, zero_s, wts, dims)

    assert x_sample.shape[1] <= SUBLANES
    y_s, a_s, q_s, s_s = _trunk(x_sample, state_conv_a[0], state_conv_qkv[0], state_delta[0], wts, dims)
    return (y_p, y_s, a_p[None], q_p[None], s_p[None], a_s[None], q_s[None], s_s[None])
```

```python
import functools

import jax
import jax.numpy as jnp
from jax import lax
from jax.experimental import pallas as pl
from jax.experimental.pallas import tpu as pltpu

EPS = 1e-6
F32 = jnp.float32
BF16 = jnp.bfloat16
LANES = 128
SUBLANES = 8
VMEM_LIMIT = 56 * 1024 * 1024

PEER_TOPK = 16
DN_CHUNK = 128
MM_PIECES = 8
NEG_INF = float("-inf")


def _cparams(sem):
    return pltpu.CompilerParams(dimension_semantics=sem, vmem_limit_bytes=VMEM_LIMIT)


def _mm(a, b):
    return jnp.dot(a.astype(BF16), b.astype(BF16), preferred_element_type=F32)


def _mm_nt(a, b):
    return lax.dot_general(a.astype(BF16), b.astype(BF16), (((1,), (1,)), ((), ())), preferred_element_type=F32)


def _mm_tn(a, b):
    return lax.dot_general(a.astype(BF16), b.astype(BF16), (((0,), (0,)), ((), ())), preferred_element_type=F32)


def _rmsnorm(x, g):
    return x * lax.rsqrt(jnp.mean(x * x, axis=-1, keepdims=True) + EPS) * g


def _sigmoid(x):
    return 1.0 / (1.0 + jnp.exp(-x))


def _silu(x):
    return x * _sigmoid(x)


def _softplus(x):
    return jnp.maximum(x, 0.0) + jnp.log(1.0 + jnp.exp(-jnp.abs(x)))


def _gelu_exact(x):
    return 0.5 * x * (1.0 + lax.erf(x * (2.0 ** -0.5)))


def _in_proj_kernel(x_ref, g_ref, w_ref, o_ref):
    xn = _rmsnorm(x_ref[...], g_ref[...]).astype(BF16)
    o_ref[...] = jnp.dot(xn, w_ref[...], preferred_element_type=F32)


def _in_proj(x2d, g, w, tm):
    n, d = x2d.shape
    width = w.shape[1]
    return pl.pallas_call(
        _in_proj_kernel,
        grid=(n // tm,),
        in_specs=[
            pl.BlockSpec((tm, d), lambda i: (i, 0)),
            pl.BlockSpec((1, d), lambda i: (0, 0)),
            pl.BlockSpec((d, width), lambda i: (0, 0)),
        ],
        out_specs=pl.BlockSpec((tm, width), lambda i: (i, 0)),
        out_shape=jax.ShapeDtypeStruct((n, width), F32),
        compiler_params=_cparams(("parallel",)),
        name="in_proj",
    )(x2d, g, w)


def _shift_rows(xp, k, rows):
    if k == 0:
        return xp[SUBLANES:SUBLANES + rows]
    return pltpu.roll(xp, k, axis=0)[SUBLANES:SUBLANES + rows]


def _cumsum_rows(x, rows):
    ridx = lax.broadcasted_iota(jnp.int32, x.shape, 0)
    s = 1
    while s < rows:
        x = x + jnp.where(ridx >= s, pltpu.roll(x, s, axis=0), 0.0)
        s *= 2
    return x


def _unit_lower_inverse_many(ms, c):
    eye = (lax.broadcasted_iota(jnp.int32, (c, c), 0) == lax.broadcasted_iota(jnp.int32, (c, c), 1)).astype(F32)
    n_factors = c.bit_length() - 1
    assert 1 << n_factors == c
    ps = [-m for m in ms]
    ts = [eye + p for p in ps]
    if n_factors == 1:
        return ts
    ps = [_mm(p, p) for p in ps]
    for _ in range(n_factors - 2):
        tps = [_mm(jnp.concatenate([t, p], axis=0), p) for t, p in zip(ts, ps)]
        ts = [t + tp[:c] for t, tp in zip(ts, tps)]
        ps = [tp[c:] for tp in tps]
    return [t + _mm(t, p) for t, p in zip(ts, ps)]


def _mixer_kernel(bch_ref, qkv_ref, z_ref, ab_ref, bufa_ref, bufq_ref, s0_ref,
                  wca_ref, wcq_ref, alog_ref, dtb_ref, gdn_ref,
                  ya_ref, og_ref, nbufa_ref, nbufq_ref, sout_ref,
                  halo_a, halo_q, state,
                  *, bb, c, heads, hd, t_real, n_chunks, unroll):
    ci = pl.program_id(1)
    cw = heads * hd
    ka = wca_ref.shape[0]
    kq = wcq_ref.shape[0]

    @pl.when(ci == 0)
    def _():
        state[...] = s0_ref[...]
        halo_a[...] = jnp.zeros(halo_a.shape, F32)
        halo_q[...] = jnp.zeros(halo_q.shape, F32)
        halo_a[:, SUBLANES - (ka - 1):, :] = bufa_ref[...]
        halo_q[:, SUBLANES - (kq - 1):, :] = bufq_ref[...]

    wca = wca_ref[...]
    wcq = wcq_ref[...]
    neg_a = -jnp.exp(alog_ref[...])
    dtb = dtb_ref[...]
    gdn = gdn_ref[...]
    rows_left = t_real - ci * c
    ridx = lax.broadcasted_iota(jnp.int32, (c, 1), 0)
    valid = ridx < rows_left
    ri = lax.broadcasted_iota(jnp.int32, (c, c), 0)
    si = lax.broadcasted_iota(jnp.int32, (c, c), 1)
    causal = ri >= si
    strict = ri > si
    cpad = max(c, LANES)

    rb = bch_ref.shape[1]

    def pad_rows(x):
        if rb == c:
            return x
        return jnp.concatenate([x, jnp.zeros((c - rb, x.shape[1]), x.dtype)], axis=0)

    def load_seq(b):
        return (pad_rows(bch_ref[b]), pad_rows(qkv_ref[b]), pad_rows(z_ref[b]), pad_rows(ab_ref[b]),
                halo_a[b], halo_q[b], [state[b, h] for h in range(heads)])

    def prep_seq(bch, qkv, z, ab, ha, hq, s_prev_all):
        b_c, c_c, h_c = bch[:, :cw], bch[:, cw:2 * cw], bch[:, 2 * cw:]
        ta = c_c * h_c
        xpa = jnp.concatenate([ha, ta], axis=0)
        conv_a = wca[ka - 1:ka] * _shift_rows(xpa, 0, c)
        for j in range(1, ka):
            conv_a = conv_a + wca[ka - 1 - j:ka - j] * _shift_rows(xpa, j, c)
        ya = b_c * conv_a
        xpq = jnp.concatenate([hq, qkv], axis=0)
        conv_q = wcq[kq - 1:kq] * _shift_rows(xpq, 0, c)
        for j in range(1, kq):
            conv_q = conv_q + wcq[kq - 1 - j:kq - j] * _shift_rows(xpq, j, c)
        conv_q = _silu(conv_q)
        new_ha, new_hq = xpa[c:c + SUBLANES], xpq[c:c + SUBLANES]
        g_all = jnp.where(valid, neg_a * _softplus(ab + dtb), 0.0)
        beta_all = jnp.where(valid, _sigmoid(ab), 0.0)
        gcum = _cumsum_rows(g_all, c)
        if c < cpad:
            gpad = jnp.concatenate([gcum, jnp.zeros((cpad - c, LANES), F32)], axis=0)
        else:
            gpad = gcum
        gcum_t = gpad.T
        chains = []
        for h in range(heads):
            q = conv_q[:, h * hd:(h + 1) * hd]
            k = conv_q[:, cw + h * hd:cw + (h + 1) * hd]
            v = conv_q[:, 2 * cw + h * hd:2 * cw + (h + 1) * hd]
            q = q * lax.rsqrt(jnp.sum(q * q, axis=-1, keepdims=True) + EPS) * (hd ** -0.5)
            k = k * lax.rsqrt(jnp.sum(k * k, axis=-1, keepdims=True) + EPS)
            q = jnp.where(valid, q, 0.0)
            k = jnp.where(valid, k, 0.0)
            v = jnp.where(valid, v, 0.0)
            gc = gcum[:, h:h + 1]
            gr = gcum_t[h:h + 1, :c]
            beta = beta_all[:, heads + h:heads + h + 1]
            g_last = gcum[c - 1:c, h:h + 1]
            eg = jnp.exp(gc)
            chains.append(dict(
                q=q, k=k, beta=beta, g_last=g_last, s_prev=s_prev_all[h], z=z[:, h * hd:(h + 1) * hd],
                decay=jnp.exp(jnp.where(causal, gc - gr, NEG_INF)),
                rhs=jnp.concatenate([v * beta, k * (beta * eg)], axis=-1),
                q_dec=q * eg, k_dec=k * jnp.exp(g_last - gc)))
        return ya, new_ha, new_hq, chains

    def delta_chains(ch):
        kkqk = [_mm_nt(jnp.concatenate([x["k"], x["q"]], axis=0), x["k"]) for x in ch]
        ms = [jnp.where(strict, kq_[:c] * x["decay"] * x["beta"], 0.0) for kq_, x in zip(kkqk, ch)]
        qks = [kq_[c:] * x["decay"] for kq_, x in zip(kkqk, ch)]
        t_invs = _unit_lower_inverse_many(ms, c)
        sols = [_mm(t, x["rhs"]) for t, x in zip(t_invs, ch)]
        ws_qs = [_mm(jnp.concatenate([sol[:, hd:], x["q_dec"]], axis=0), x["s_prev"])
                 for sol, x in zip(sols, ch)]
        v_news = [sol[:, :hd] - wq[:c] for sol, wq in zip(sols, ws_qs)]
        if c >= LANES:
            rs = [_mm(jnp.concatenate([qk, x["k_dec"].T], axis=0), vn)
                  for qk, x, vn in zip(qks, ch, v_news)]
            os_ = [wq[c:] + r[:c] for wq, r in zip(ws_qs, rs)]
            upd = [r[c:] for r in rs]
        else:
            os_ = [wq[c:] + _mm(qk, vn) for wq, qk, vn in zip(ws_qs, qks, v_news)]
            upd = [_mm_tn(x["k_dec"], vn) for x, vn in zip(ch, v_news)]
        states = [x["s_prev"] * jnp.exp(x["g_last"]) + u_ for x, u_ in zip(ch, upd)]
        outs = [o * lax.rsqrt(jnp.mean(o * o, axis=-1, keepdims=True) + EPS) * gdn * _silu(x["z"])
                for o, x in zip(os_, ch)]
        return outs, states

    def store_seq(b, ya, og, new_ha, new_hq, new_states):
        ya_ref[b] = ya[:rb]
        og_ref[b] = og[:rb]
        halo_a[b] = new_ha
        halo_q[b] = new_hq
        for h in range(heads):
            state[b, h] = new_states[h]

    def per_group(gi, carry):
        b0 = gi * unroll
        preps = [prep_seq(*load_seq(b0 + j)) for j in range(unroll)]
        outs, states = delta_chains([x for pr in preps for x in pr[3]])
        for j, (ya, new_ha, new_hq, _) in enumerate(preps):
            og = jnp.concatenate(outs[j * heads:(j + 1) * heads], axis=-1)
            store_seq(b0 + j, ya, og, new_ha, new_hq, states[j * heads:(j + 1) * heads])
        return carry

    lax.fori_loop(0, bb // unroll, per_group, 0)

    @pl.when(ci == n_chunks - 1)
    def _():
        sout_ref[...] = state[...]
        last = t_real - (n_chunks - 1) * c
        nbufa_ref[...] = (bch_ref[:, last - (ka - 1):last, cw:2 * cw]
                          * bch_ref[:, last - (ka - 1):last, 2 * cw:])
        nbufq_ref[...] = qkv_ref[:, last - (kq - 1):last, :]


def _mixer(proj3, buf_a, buf_q, s0, w_conv_a, w_conv_q, a_log_pad, dt_pad, g_dn, *, bb, c, t_real, heads, hd,
           unroll):
    bsz, tp, _ = proj3.shape
    cw = heads * hd
    n_chunks = pl.cdiv(tp, c)
    rb = min(c, tp)
    assert n_chunks * rb == tp
    ka, kq = w_conv_a.shape[0], w_conv_q.shape[0]
    kern = functools.partial(_mixer_kernel, bb=bb, c=c, heads=heads, hd=hd, t_real=t_real, n_chunks=n_chunks,
                             unroll=unroll)
    return pl.pallas_call(
        kern,
        grid=(bsz // bb, n_chunks),
        in_specs=[
            pl.BlockSpec((bb, rb, 3 * cw), lambda i, j: (i, j, 0)),
            pl.BlockSpec((bb, rb, 3 * cw), lambda i, j: (i, j, 1)),
            pl.BlockSpec((bb, rb, cw), lambda i, j: (i, j, (3 * cw + 3 * cw + 4 * cw) // cw)),
            pl.BlockSpec((bb, rb, LANES), lambda i, j: (i, j, (3 * cw + 3 * cw + 4 * cw + cw) // LANES)),
            pl.BlockSpec((bb, ka - 1, cw), lambda i, j: (i, 0, 0)),
            pl.BlockSpec((bb, kq - 1, 3 * cw), lambda i, j: (i, 0, 0)),
            pl.BlockSpec((bb, heads, hd, hd), lambda i, j: (i, 0, 0, 0)),
            pl.BlockSpec((ka, cw), lambda i, j: (0, 0)),
            pl.BlockSpec((kq, 3 * cw), lambda i, j: (0, 0)),
            pl.BlockSpec((1, LANES), lambda i, j: (0, 0)),
            pl.BlockSpec((1, LANES), lambda i, j: (0, 0)),
            pl.BlockSpec((1, hd), lambda i, j: (0, 0)),
        ],
        out_specs=[
            pl.BlockSpec((bb, rb, cw), lambda i, j: (i, j, 0)),
            pl.BlockSpec((bb, rb, cw), lambda i, j: (i, j, 0)),
            pl.BlockSpec((bb, ka - 1, cw), lambda i, j: (i, 0, 0)),
            pl.BlockSpec((bb, kq - 1, 3 * cw), lambda i, j: (i, 0, 0)),
            pl.BlockSpec((bb, heads, hd, hd), lambda i, j: (i, 0, 0, 0)),
        ],
        out_shape=[
            jax.ShapeDtypeStruct((bsz, tp, cw), F32),
            jax.ShapeDtypeStruct((bsz, tp, cw), F32),
            jax.ShapeDtypeStruct((bsz, ka - 1, cw), F32),
            jax.ShapeDtypeStruct((bsz, kq - 1, 3 * cw), F32),
            jax.ShapeDtypeStruct((bsz, heads, hd, hd), F32),
        ],
        scratch_shapes=[
            pltpu.VMEM((bb, SUBLANES, cw), F32),
            pltpu.VMEM((bb, SUBLANES, 3 * cw), F32),
            pltpu.VMEM((bb, heads, hd, hd), F32),
        ],
        compiler_params=_cparams(("parallel", "arbitrary")),
        name="mixer",
    )(proj3, proj3, proj3, proj3, buf_a, buf_q, s0, w_conv_a, w_conv_q, a_log_pad, dt_pad, g_dn)


def _out_proj_kernel(x_ref, ya_ref, og_ref, ga_ref, gb_ref, woa_ref, wod_ref, wo_ref, h_ref):
    y_a = jnp.dot(ya_ref[...].astype(BF16), woa_ref[...], preferred_element_type=F32)
    y_b = jnp.dot(og_ref[...].astype(BF16), wod_ref[...], preferred_element_type=F32)
    m = _sigmoid(ga_ref[...]) * y_a + _sigmoid(gb_ref[...]) * y_b
    h_ref[...] = x_ref[...] + jnp.dot(m.astype(BF16), wo_ref[...], preferred_element_type=F32)


def _out_proj(x2d, ya, og, proj, w_out_a, w_out_dn, w_o, tm, gate_block):
    n, d = x2d.shape
    cw = ya.shape[1]
    return pl.pallas_call(
        _out_proj_kernel,
        grid=(n // tm,),
        in_specs=[
            pl.BlockSpec((tm, d), lambda i: (i, 0)),
            pl.BlockSpec((tm, cw), lambda i: (i, 0)),
            pl.BlockSpec((tm, cw), lambda i: (i, 0)),
            pl.BlockSpec((tm, d), lambda i: (i, gate_block)),
            pl.BlockSpec((tm, d), lambda i: (i, gate_block + 1)),
            pl.BlockSpec((cw, d), lambda i: (0, 0)),
            pl.BlockSpec((cw, d), lambda i: (0, 0)),
            pl.BlockSpec((d, d), lambda i: (0, 0)),
        ],
        out_specs=pl.BlockSpec((tm, d), lambda i: (i, 0)),
        out_shape=jax.ShapeDtypeStruct((n, d), F32),
        compiler_params=_cparams(("parallel",)),
        name="out_proj",
    )(x2d, ya, og, proj, proj, w_out_a, w_out_dn, w_o)


def _pack_u_kernel(u_ref, o_ref):
    o_ref[...] = pltpu.bitcast(u_ref[...].astype(BF16), jnp.uint32)


def _pack_vt_kernel(v_ref, o_ref):
    o_ref[0] = pltpu.bitcast(v_ref[...].T.astype(BF16), jnp.uint32)


def _pack_experts(expert_u, expert_v, rows):
    n_exp, d = expert_u.shape
    u_pk = pl.pallas_call(
        _pack_u_kernel,
        grid=(n_exp // rows,),
        in_specs=[pl.BlockSpec((rows, d), lambda i: (i, 0))],
        out_specs=pl.BlockSpec((rows // 2, d), lambda i: (i, 0)),
        out_shape=jax.ShapeDtypeStruct((n_exp // 2, d), jnp.uint32),
        compiler_params=_cparams(("parallel",)),
        name="pack_u",
    )(expert_u)
    vt_pk = pl.pallas_call(
        _pack_vt_kernel,
        grid=(n_exp // rows,),
        in_specs=[pl.BlockSpec((rows, d), lambda i: (i, 0))],
        out_specs=pl.BlockSpec((1, d // 2, rows), lambda i: (i, 0, 0)),
        out_shape=jax.ShapeDtypeStruct((n_exp // rows, d // 2, rows), jnp.uint32),
        compiler_params=_cparams(("parallel",)),
        name="pack_vt",
    )(expert_v)
    return u_pk, vt_pk


def _extract_topk(ss, flat, topk, exact_ties):
    n, t = ss[0].shape
    big = jnp.int32(2 ** 30)
    ro = lax.broadcasted_iota(jnp.int32, (topk, t), 0)
    cur = list(ss)
    ranks = [jnp.full((n, t), float(topk), F32) for _ in ss]
    vals = [jnp.zeros((topk, t), F32) for _ in ss]
    for r in range(topk):
        for i in range(len(ss)):
            m = jnp.max(cur[i], axis=0, keepdims=True)
            hit = cur[i] == m
            if exact_ties:
                hit = flat == jnp.min(jnp.where(hit, flat, big), axis=0, keepdims=True)
            ranks[i] = jnp.where(hit, float(r), ranks[i])
            cur[i] = jnp.where(hit, NEG_INF, cur[i])
            vals[i] = jnp.where(ro == r, m, vals[i])
    return vals, ranks


def _tie_count(ranks, topk):
    bad = None
    for rk in ranks:
        n_marked = jnp.sum(jnp.where(rk < topk, 1.0, 0.0), axis=0, keepdims=True)
        bad = jnp.abs(n_marked - topk) if bad is None else bad + jnp.abs(n_marked - topk)
    return bad


def _pair_candidates(v1, v2, topk):
    io8 = lax.broadcasted_iota(jnp.int32, (SUBLANES, 1), 0)
    pieces = [v1[0:1] + v2]
    flats = [lax.broadcasted_iota(jnp.int32, (topk, 1), 0)]
    for a in range(1, SUBLANES):
        pieces.append(jnp.where(io8 < (topk // (a + 1)), v1[a:a + 1] + v2[0:SUBLANES], NEG_INF))
        flats.append(a * topk + io8)
    pieces.append(v1[SUBLANES:] + v2[0:1])
    flats.append((SUBLANES + io8) * topk)
    return jnp.concatenate(pieces, axis=0), jnp.concatenate(flats, axis=0)


def _pair_counts(sums, rank, mtot, topk):
    t = sums.shape[1]
    sel = jnp.where(rank < topk, 1.0, 0.0)
    z = jnp.sum(sel * jnp.exp(jnp.where(rank < topk, sums, mtot) - mtot), axis=0, keepdims=True)
    ro = lax.broadcasted_iota(jnp.int32, (topk, t), 0)
    nb = jnp.zeros((topk, t), F32)
    nb = jnp.where(ro == 0, jnp.sum(sel[0:topk], axis=0, keepdims=True), nb)
    for a in range(1, SUBLANES):
        lo = topk + (a - 1) * SUBLANES
        nb = jnp.where(ro == a, jnp.sum(sel[lo:lo + SUBLANES], axis=0, keepdims=True), nb)
    tail = sel[topk + (SUBLANES - 1) * SUBLANES:]
    nb = jnp.where(ro >= SUBLANES, jnp.concatenate([tail, tail], axis=0), nb)
    return nb, z


def _peer_select_kernel(h_ref, g_ref, wqt_ref, keys_ref, xn_ref, r2_ref, e2_ref, cnt_ref, c1_ref, q_scr,
                        *, heads, half, topk, group):
    xf = _rmsnorm(h_ref[...], g_ref[...])
    xn = xf.astype(BF16)
    xn_ref[...] = pltpu.bitcast(xf.T.astype(BF16), jnp.uint32)
    q_scr[...] = lax.dot_general(wqt_ref[...], xn, (((1,), (1,)), ((), ())), preferred_element_type=F32)
    nkeys = keys_ref.shape[1]
    key_idx = lax.broadcasted_iota(jnp.int32, (nkeys, 1), 0)

    def select_heads(exact_ties):
        def per_group(gi, bad):
            hs = [gi * group + j for j in range(group)]
            scores = []
            for h in hs:
                base = pl.multiple_of(h * 2 * half, 2 * half)
                scores.append(_mm(keys_ref[h], q_scr[pl.ds(base, half), :]))
                scores.append(_mm(keys_ref[heads + h], q_scr[pl.ds(base + half, half), :]))
            vals, ranks = _extract_topk(scores, key_idx, topk, exact_ties)
            cands = [_pair_candidates(vals[2 * j], vals[2 * j + 1], topk) for j in range(group)]
            _, pair_ranks = _extract_topk([cd[0] for cd in cands], cands[0][1], topk, exact_ties)
            for j, h in enumerate(hs):
                v1, r1, v2, r2 = vals[2 * j], ranks[2 * j], vals[2 * j + 1], ranks[2 * j + 1]
                nb, z = _pair_counts(cands[j][0], pair_ranks[j], v1[0:1] + v2[0:1], topk)
                cnt = jnp.zeros(r1.shape, F32)
                for a in range(topk):
                    cnt = jnp.where(r1 == float(a), nb[a:a + 1], cnt)
                r2_ref[h, 0] = pltpu.bitcast(r2.astype(BF16), jnp.uint32)
                e2_ref[h, 0] = pltpu.bitcast(jnp.exp(scores[2 * j + 1] - v2[0:1]).astype(BF16), jnp.uint32)
                cnt_ref[h, 0] = cnt
                c1_ref[h, 0] = jnp.exp(scores[2 * j] - v1[0:1]) / z
            return bad + _tie_count(ranks + pair_ranks, topk)

        return lax.fori_loop(0, heads // group, per_group, jnp.zeros((1, h_ref.shape[0]), F32))

    ties = select_heads(exact_ties=False)

    @pl.when(jnp.max(ties) > 0.0)
    def _():
        select_heads(exact_ties=True)


def _peer_select(h2d, g, wq_t, keys, *, heads, nkeys, half, topk, tms):
    n, d = h2d.shape
    qd = wq_t.shape[0]
    kern = functools.partial(_peer_select_kernel, heads=heads, half=half, topk=topk, group=4)
    assert tms == LANES
    sel_f32 = jax.ShapeDtypeStruct((heads, n // LANES, nkeys, LANES), F32)
    sel_pk = jax.ShapeDtypeStruct((heads, n // LANES, nkeys // 2, LANES), jnp.uint32)
    sel_spec = pl.BlockSpec((heads, 1, nkeys, LANES), lambda i: (0, i, 0, 0))
    pk_spec = pl.BlockSpec((heads, 1, nkeys // 2, LANES), lambda i: (0, i, 0, 0))
    return pl.pallas_call(
        kern,
        grid=(n // tms,),
        in_specs=[
            pl.BlockSpec((tms, d), lambda i: (i, 0)),
            pl.BlockSpec((1, d), lambda i: (0, 0)),
            pl.BlockSpec((qd, d), lambda i: (0, 0)),
            pl.BlockSpec(keys.shape, lambda i: (0, 0, 0)),
        ],
        out_specs=[pl.BlockSpec((d // 2, tms), lambda i: (0, i)), pk_spec, pk_spec, sel_spec, sel_spec],
        out_shape=[jax.ShapeDtypeStruct((d // 2, n), jnp.uint32), sel_pk, sel_pk, sel_f32, sel_f32],
        scratch_shapes=[pltpu.VMEM((qd, tms), F32)],
        compiler_params=_cparams(("parallel",)),
        name="peer_select",
    )(h2d, g, wq_t, keys)


def _pair_index(s, offset, n_pairs):
    return jnp.clip(s - offset, 0, n_pairs - 1)


def _peer_dense_kernel(xn_ref, u_ref, vt_ref, r2_ref, e2_ref, cnt_ref, c1_ref, h_ref, g_ref, y_ref,
                       acc_ref, ht0_ref, ht1_ref, w0_ref, w1_ref, *, heads, nkeys, eb, tm, n_eb, n_pairs):
    s = pl.program_id(0)
    eb_b = lax.rem(_pair_index(s, 1, n_pairs), n_eb)
    eb_c = lax.rem(_pair_index(s, 2, n_pairs), n_eb)

    @pl.when(s == 0)
    def _():
        ht1_ref[...] = jnp.zeros(ht1_ref.shape, F32)
        w0_ref[...] = jnp.zeros(w0_ref.shape, jnp.uint32)
        w1_ref[...] = jnp.zeros(w1_ref.shape, jnp.uint32)

    @pl.when(eb_c == 0)
    def _():
        acc_ref[...] = jnp.zeros(acc_ref.shape, F32)

    i0 = pl.multiple_of(eb_b * (eb // nkeys), SUBLANES)
    zero = jnp.zeros((nkeys, LANES), BF16)

    def step(ht_cur, ht_prev, w_cur, w_prev):
        d = acc_ref.shape[0]
        xn_t = pltpu.bitcast(xn_ref[...], BF16)
        w_in = jnp.concatenate([pltpu.bitcast(w_prev[lc], BF16) for lc in range(tm // LANES)], axis=1)
        for q in range(MM_PIECES):
            dr = slice(q * (d // MM_PIECES), (q + 1) * (d // MM_PIECES))
            dr2 = slice(q * (d // 2 // MM_PIECES), (q + 1) * (d // 2 // MM_PIECES))
            acc_ref[dr, :] += jnp.dot(pltpu.bitcast(vt_ref[0, dr2, :], BF16), w_in, preferred_element_type=F32)
            rows = eb // nkeys // MM_PIECES
            for ii in range(q * rows, (q + 1) * rows):
                rs = slice(ii * nkeys, (ii + 1) * nkeys)
                rs2 = slice(ii * (nkeys // 2), (ii + 1) * (nkeys // 2))
                for lc in range(tm // LANES):
                    gate = zero
                    for h in range(heads):
                        cnt = cnt_ref[h, lc, pl.ds(i0, SUBLANES), :][ii:ii + 1]
                        c1 = c1_ref[h, lc, pl.ds(i0, SUBLANES), :][ii:ii + 1]
                        cnt = jnp.broadcast_to(cnt, (nkeys, LANES)).astype(BF16)
                        c1 = jnp.broadcast_to(c1, (nkeys, LANES)).astype(BF16)
                        r2 = pltpu.bitcast(r2_ref[h, lc], BF16)
                        e2 = pltpu.bitcast(e2_ref[h, lc], BF16)
                        gate = gate + jnp.where(r2 < cnt, e2, zero) * c1
                    act = _gelu_exact(ht_prev[lc, rs, :]).astype(BF16)
                    w_cur[lc, rs2, :] = pltpu.bitcast(gate * act, jnp.uint32)
            er = slice(q * (eb // MM_PIECES), (q + 1) * (eb // MM_PIECES))
            er2 = slice(q * (eb // 2 // MM_PIECES), (q + 1) * (eb // 2 // MM_PIECES))
            res = jnp.dot(pltpu.bitcast(u_ref[er2, :], BF16), xn_t, preferred_element_type=F32)
            for lc in range(tm // LANES):
                ht_cur[lc, er, :] = res[:, lc * LANES:(lc + 1) * LANES]

    parity = lax.rem(s, 2)

    @pl.when(parity == 0)
    def _():
        step(ht0_ref, ht1_ref, w0_ref, w1_ref)

    @pl.when(parity == 1)
    def _():
        step(ht1_ref, ht0_ref, w1_ref, w0_ref)

    @pl.when(jnp.logical_and(s >= 2, eb_c == n_eb - 1))
    def _():
        y_ref[...] = _rmsnorm(h_ref[...] + acc_ref[...].T, g_ref[...])


def _peer_dense(xn, u, vt, r2, e2, cnt, c1, h2d, g, *, heads, nkeys, eb, tm):
    n, d = h2d.shape
    n_exp = 2 * u.shape[0]
    assert eb == SUBLANES * nkeys
    assert vt.shape == (n_exp // eb, d // 2, eb)
    n_eb = n_exp // eb
    n_pairs = (n // tm) * n_eb
    kern = functools.partial(_peer_dense_kernel, heads=heads, nkeys=nkeys, eb=eb, tm=tm, n_eb=n_eb, n_pairs=n_pairs)

    def tb(offset):
        return lambda s: _pair_index(s, offset, n_pairs) // n_eb

    def ebk(offset):
        return lambda s: lax.rem(_pair_index(s, offset, n_pairs), n_eb)

    sel_spec = pl.BlockSpec((heads, tm // LANES, nkeys, LANES), lambda s: (0, tb(1)(s), 0, 0))
    pk_spec = pl.BlockSpec((heads, tm // LANES, nkeys // 2, LANES), lambda s: (0, tb(1)(s), 0, 0))
    return pl.pallas_call(
        kern,
        grid=(n_pairs + 2,),
        in_specs=[
            pl.BlockSpec((d // 2, tm), lambda s: (0, tb(0)(s))),
            pl.BlockSpec((eb // 2, d), lambda s: (ebk(0)(s), 0)),
            pl.BlockSpec((1, d // 2, eb), lambda s: (ebk(2)(s), 0, 0)),
            pk_spec, pk_spec, sel_spec, sel_spec,
            pl.BlockSpec((tm, d), lambda s: (tb(2)(s), 0)),
            pl.BlockSpec((1, d), lambda s: (0, 0)),
        ],
        out_specs=pl.BlockSpec((tm, d), lambda s: (tb(2)(s), 0)),
        out_shape=jax.ShapeDtypeStruct((n, d), F32),
        scratch_shapes=[
            pltpu.VMEM((d, tm), F32),
            pltpu.VMEM((tm // LANES, eb, LANES), F32),
            pltpu.VMEM((tm // LANES, eb, LANES), F32),
            pltpu.VMEM((tm // LANES, eb // 2, LANES), jnp.uint32),
            pltpu.VMEM((tm // LANES, eb // 2, LANES), jnp.uint32),
        ],
        compiler_params=_cparams(("arbitrary",)),
        name="peer_dense",
    )(xn, u, vt, r2, e2, cnt, c1, h2d, g)


def _pick_block(n, candidates):
    for c in candidates:
        if n % c == 0:
            return c
    raise ValueError(f"no block size in {candidates} divides {n}")


def _trunk(x, buf_a, buf_q, s0, wts, dims):
    heads, hd, cw, p_heads, nkeys, half = dims
    (g_mix, w_in, w_conv_a, w_conv_q, a_log_pad, dt_pad, g_dn, w_out_a, w_out_dn, w_o, g_ffn,
     wq_t, keys, u_pk, vt_pk, g_final) = wts
    bsz, tp, d = x.shape
    n = bsz * tp
    x2d = x.reshape(n, d)
    c = DN_CHUNK if tp % DN_CHUNK == 0 else SUBLANES
    ka, kq = w_conv_a.shape[0], w_conv_q.shape[0]
    last = tp - (pl.cdiv(tp, c) - 1) * c
    assert (tp % c == 0 or tp < c) and last >= kq - 1 and last >= ka - 1, (tp, c)

    proj = _in_proj(x2d, g_mix, w_in, _pick_block(n, (256, 128)))
    if c == DN_CHUNK:
        bb, unroll = _pick_block(bsz, (4, 2, 1)), 2
    else:
        bb, unroll = _pick_block(bsz, (16, 8, 4, 2, 1)), 16
    ya, og, nbuf_a, nbuf_q, s_new = _mixer(
        proj.reshape(bsz, tp, -1), buf_a, buf_q, s0, w_conv_a, w_conv_q, a_log_pad, dt_pad, g_dn,
        bb=bb, c=c, t_real=tp, heads=heads, hd=hd, unroll=min(unroll, bb))
    tm2 = _pick_block(n, (512, 256, 128))
    hres = _out_proj(x2d, ya.reshape(n, cw), og.reshape(n, cw), proj, w_out_a, w_out_dn, w_o, tm2,
                     gate_block=(6 * cw) // d)
    xn_t, r2, e2, cnt, c1 = _peer_select(hres, g_ffn, wq_t, keys, heads=p_heads, nkeys=nkeys, half=half,
                                         topk=PEER_TOPK, tms=LANES)
    tm3 = _pick_block(n, (512, 256, 128))
    y = _peer_dense(xn_t, u_pk, vt_pk, r2, e2, cnt, c1, hres, g_final, heads=p_heads, nkeys=nkeys,
                    eb=SUBLANES * nkeys, tm=tm3)
    return y.reshape(bsz, tp, d), nbuf_a, nbuf_q, s_new


def kernel(x_prompt, x_sample, state_conv_a, state_conv_qkv, state_delta, g_norm_mix, w_in, w_conv_a, w_conv_qkv,
           a_log, dt_bias, g_dn_norm, w_out_a, w_out_dn, w_o, g_norm_ffn, w_query, sub_keys, expert_u, expert_v,
           g_norm_final):
    depth = w_in.shape[0]
    assert depth == 1
    d = x_prompt.shape[-1]
    heads = a_log.shape[1]
    hd = g_dn_norm.shape[1]
    cw = heads * hd
    assert w_conv_a.shape[2] == cw and w_conv_qkv.shape[2] == 3 * cw and d == 2 * cw
    p_heads, nkeys, half = sub_keys.shape[2], sub_keys.shape[3], sub_keys.shape[4]
    assert nkeys == LANES and expert_u.shape[1] == nkeys * nkeys

    w = w_in[0]
    o_qkv, o_z, o_a, o_b, o_g = 3 * cw, 6 * cw, 7 * cw, 7 * cw + heads, 7 * cw + 2 * heads
    ab_pad = jnp.zeros((d, LANES - 2 * heads), w.dtype)
    w_r = jnp.concatenate([w[:, :o_z], w[:, o_g:], w[:, o_z:o_a], w[:, o_a:o_g], ab_pad], axis=1).astype(BF16)
    pad4 = lambda v: jnp.pad(v.astype(F32), ((0, 0), (0, LANES - v.shape[1])))
    wts = (
        g_norm_mix, w_r, w_conv_a[0], w_conv_qkv[0], pad4(a_log), pad4(dt_bias), g_dn_norm,
        w_out_a[0].astype(BF16), w_out_dn[0].astype(BF16), w_o[0].astype(BF16), g_norm_ffn,
        w_query[0].T.astype(BF16), sub_keys[0].reshape(2 * p_heads, nkeys, half).astype(BF16),
        *_pack_experts(expert_u[0], expert_v[0], SUBLANES * nkeys), g_norm_final.reshape(1, d),
    )
    dims = (heads, hd, cw, p_heads, nkeys, half)

    bp, tpr, _ = x_prompt.shape
    assert tpr % DN_CHUNK == 0
    ka, kq = w_conv_a.shape[1], w_conv_qkv.shape[1]
    zero_a = jnp.zeros((bp, ka - 1, cw), F32)
    zero_q = jnp.zeros((bp, kq - 1, 3 * cw), F32)
    zero_s = jnp.zeros((bp, heads, hd, hd), F32)
    y_p, a_p, q_p, s_p = _trunk(x_prompt, zero_a, zero_q, zero_s, wts, dims)

    assert x_sample.shape[1] <= SUBLANES
    y_s, a_s, q_s, s_s = _trunk(x_sample, state_conv_a[0], state_conv_qkv[0], state_delta[0], wts, dims)
    return (y_p, y_s, a_p[None], q_p[None], s_p[None], a_s[None], q_s[None], s_s[None])
```

```python
import functools

import jax
import jax.numpy as jnp
from jax import lax
from jax.experimental import pallas as pl
from jax.experimental.pallas import tpu as pltpu

EPS = 1e-6
F32 = jnp.float32
BF16 = jnp.bfloat16
LANES = 128
SUBLANES = 8
VMEM_LIMIT = 56 * 1024 * 1024

PEER_TOPK = 16
DN_CHUNK = 128
MM_PIECES = 8
NEG_INF = float("-inf")


def _cparams(sem):
    return pltpu.CompilerParams(dimension_semantics=sem, vmem_limit_bytes=VMEM_LIMIT)


def _mm(a, b):
    return jnp.dot(a.astype(BF16), b.astype(BF16), preferred_element_type=F32)


def _mm_nt(a, b):
    return lax.dot_general(a.astype(BF16), b.astype(BF16), (((1,), (1,)), ((), ())), preferred_element_type=F32)


def _mm_tn(a, b):
    return lax.dot_general(a.astype(BF16), b.astype(BF16), (((0,), (0,)), ((), ())), preferred_element_type=F32)


def _rmsnorm(x, g):
    return x * lax.rsqrt(jnp.mean(x * x, axis=-1, keepdims=True) + EPS) * g


def _sigmoid(x):
    return 1.0 / (1.0 + jnp.exp(-x))


def _silu(x):
    return x * _sigmoid(x)


def _softplus(x):
    return jnp.maximum(x, 0.0) + jnp.log(1.0 + jnp.exp(-jnp.abs(x)))


def _gelu_exact(x):
    return 0.5 * x * (1.0 + lax.erf(x * (2.0 ** -0.5)))


def _in_proj_kernel(x_ref, g_ref, w_ref, o_ref):
    xn = _rmsnorm(x_ref[...], g_ref[...]).astype(BF16)
    o_ref[...] = jnp.dot(xn, w_ref[...], preferred_element_type=F32)


def _in_proj(x2d, g, w, tm):
    n, d = x2d.shape
    width = w.shape[1]
    return pl.pallas_call(
        _in_proj_kernel,
        grid=(n // tm,),
        in_specs=[
            pl.BlockSpec((tm, d), lambda i: (i, 0)),
            pl.BlockSpec((1, d), lambda i: (0, 0)),
            pl.BlockSpec((d, width), lambda i: (0, 0)),
        ],
        out_specs=pl.BlockSpec((tm, width), lambda i: (i, 0)),
        out_shape=jax.ShapeDtypeStruct((n, width), F32),
        compiler_params=_cparams(("parallel",)),
        name="in_proj",
    )(x2d, g, w)


def _shift_rows(xp, k, rows):
    if k == 0:
        return xp[SUBLANES:SUBLANES + rows]
    return pltpu.roll(xp, k, axis=0)[SUBLANES:SUBLANES + rows]


def _cumsum_rows(x, rows):
    ridx = lax.broadcasted_iota(jnp.int32, x.shape, 0)
    s = 1
    while s < rows:
        x = x + jnp.where(ridx >= s, pltpu.roll(x, s, axis=0), 0.0)
        s *= 2
    return x


def _unit_lower_inverse_many(ms, c):
    eye = (lax.broadcasted_iota(jnp.int32, (c, c), 0) == lax.broadcasted_iota(jnp.int32, (c, c), 1)).astype(F32)
    n_factors = c.bit_length() - 1
    assert 1 << n_factors == c
    ps = [-m for m in ms]
    ts = [eye + p for p in ps]
    if n_factors == 1:
        return ts
    ps = [_mm(p, p) for p in ps]
    for _ in range(n_factors - 2):
        tps = [_mm(jnp.concatenate([t, p], axis=0), p) for t, p in zip(ts, ps)]
        ts = [t + tp[:c] for t, tp in zip(ts, tps)]
        ps = [tp[c:] for tp in tps]
    return [t + _mm(t, p) for t, p in zip(ts, ps)]


def _mixer_kernel(bch_ref, qkv_ref, z_ref, ab_ref, bufa_ref, bufq_ref, s0_ref,
                  wca_ref, wcq_ref, alog_ref, dtb_ref, gdn_ref,
                  ya_ref, og_ref, nbufa_ref, nbufq_ref, sout_ref,
                  halo_a, halo_q, state,
                  *, bb, c, heads, hd, t_real, n_chunks, unroll):
    ci = pl.program_id(1)
    cw = heads * hd
    ka = wca_ref.shape[0]
    kq = wcq_ref.shape[0]

    @pl.when(ci == 0)
    def _():
        state[...] = s0_ref[...]
        halo_a[...] = jnp.zeros(halo_a.shape, F32)
        halo_q[...] = jnp.zeros(halo_q.shape, F32)
        halo_a[:, SUBLANES - (ka - 1):, :] = bufa_ref[...]
        halo_q[:, SUBLANES - (kq - 1):, :] = bufq_ref[...]

    wca = wca_ref[...]
    wcq = wcq_ref[...]
    neg_a = -jnp.exp(alog_ref[...])
    dtb = dtb_ref[...]
    gdn = gdn_ref[...]
    rows_left = t_real - ci * c
    ridx = lax.broadcasted_iota(jnp.int32, (c, 1), 0)
    valid = ridx < rows_left
    ri = lax.broadcasted_iota(jnp.int32, (c, c), 0)
    si = lax.broadcasted_iota(jnp.int32, (c, c), 1)
    causal = ri >= si
    strict = ri > si
    cpad = max(c, LANES)

    rb = bch_ref.shape[1]

    def pad_rows(x):
        if rb == c:
            return x
        return jnp.concatenate([x, jnp.zeros((c - rb, x.shape[1]), x.dtype)], axis=0)

    def load_seq(b):
        return (pad_rows(bch_ref[b]), pad_rows(qkv_ref[b]), pad_rows(z_ref[b]), pad_rows(ab_ref[b]),
                halo_a[b], halo_q[b], [state[b, h] for h in range(heads)])

    def prep_seq(bch, qkv, z, ab, ha, hq, s_prev_all):
        b_c, c_c, h_c = bch[:, :cw], bch[:, cw:2 * cw], bch[:, 2 * cw:]
        ta = c_c * h_c
        xpa = jnp.concatenate([ha, ta], axis=0)
        conv_a = wca[ka - 1:ka] * _shift_rows(xpa, 0, c)
        for j in range(1, ka):
            conv_a = conv_a + wca[ka - 1 - j:ka - j] * _shift_rows(xpa, j, c)
        ya = b_c * conv_a
        xpq = jnp.concatenate([hq, qkv], axis=0)
        conv_q = wcq[kq - 1:kq] * _shift_rows(xpq, 0, c)
        for j in range(1, kq):
            conv_q = conv_q + wcq[kq - 1 - j:kq - j] * _shift_rows(xpq, j, c)
        conv_q = _silu(conv_q)
        new_ha, new_hq = xpa[c:c + SUBLANES], xpq[c:c + SUBLANES]
        g_all = jnp.where(valid, neg_a * _softplus(ab + dtb), 0.0)
        beta_all = jnp.where(valid, _sigmoid(ab), 0.0)
        gcum = _cumsum_rows(g_all, c)
        if c < cpad:
            gpad = jnp.concatenate([gcum, jnp.zeros((cpad - c, LANES), F32)], axis=0)
        else:
            gpad = gcum
        gcum_t = gpad.T
        chains = []
        for h in range(heads):
            q = conv_q[:, h * hd:(h + 1) * hd]
            k = conv_q[:, cw + h * hd:cw + (h + 1) * hd]
            v = conv_q[:, 2 * cw + h * hd:2 * cw + (h + 1) * hd]
            q = q * lax.rsqrt(jnp.sum(q * q, axis=-1, keepdims=True) + EPS) * (hd ** -0.5)
            k = k * lax.rsqrt(jnp.sum(k * k, axis=-1, keepdims=True) + EPS)
            q = jnp.where(valid, q, 0.0)
            k = jnp.where(valid, k, 0.0)
            v = jnp.where(valid, v, 0.0)
            gc = gcum[:, h:h + 1]
            gr = gcum_t[h:h + 1, :c]
            beta = beta_all[:, heads + h:heads + h + 1]
            g_last = gcum[c - 1:c, h:h + 1]
            eg = jnp.exp(gc)
            chains.append(dict(
                q=q, k=k, beta=beta, g_last=g_last, s_prev=s_prev_all[h], z=z[:, h * hd:(h + 1) * hd],
                decay=jnp.exp(jnp.where(causal, gc - gr, NEG_INF)),
                rhs=jnp.concatenate([v * beta, k * (beta * eg)], axis=-1),
                q_dec=q * eg, k_dec=k * jnp.exp(g_last - gc)))
        return ya, new_ha, new_hq, chains

    def delta_chains(ch):
        kkqk = [_mm_nt(jnp.concatenate([x["k"], x["q"]], axis=0), x["k"]) for x in ch]
        ms = [jnp.where(strict, kq_[:c] * x["decay"] * x["beta"], 0.0) for kq_, x in zip(kkqk, ch)]
        qks = [kq_[c:] * x["decay"] for kq_, x in zip(kkqk, ch)]
        t_invs = _unit_lower_inverse_many(ms, c)
        sols = [_mm(t, x["rhs"]) for t, x in zip(t_invs, ch)]
        ws_qs = [_mm(jnp.concatenate([sol[:, hd:], x["q_dec"]], axis=0), x["s_prev"])
                 for sol, x in zip(sols, ch)]
        v_news = [sol[:, :hd] - wq[:c] for sol, wq in zip(sols, ws_qs)]
        if c >= LANES:
            rs = [_mm(jnp.concatenate([qk, x["k_dec"].T], axis=0), vn)
                  for qk, x, vn in zip(qks, ch, v_news)]
            os_ = [wq[c:] + r[:c] for wq, r in zip(ws_qs, rs)]
            upd = [r[c:] for r in rs]
        else:
            os_ = [wq[c:] + _mm(qk, vn) for wq, qk, vn in zip(ws_qs, qks, v_news)]
            upd = [_mm_tn(x["k_dec"], vn) for x, vn in zip(ch, v_news)]
        states = [x["s_prev"] * jnp.exp(x["g_last"]) + u_ for x, u_ in zip(ch, upd)]
        outs = [o * lax.rsqrt(jnp.mean(o * o, axis=-1, keepdims=True) + EPS) * gdn * _silu(x["z"])
                for o, x in zip(os_, ch)]
        return outs, states

    def store_seq(b, ya, og, new_ha, new_hq, new_states):
        ya_ref[b] = ya[:rb]
        og_ref[b] = og[:rb]
        halo_a[b] = new_ha
        halo_q[b] = new_hq
        for h in range(heads):
            state[b, h] = new_states[h]

    def per_group(gi, carry):
        b0 = gi * unroll
        preps = [prep_seq(*load_seq(b0 + j)) for j in range(unroll)]
        outs, states = delta_chains([x for pr in preps for x in pr[3]])
        for j, (ya, new_ha, new_hq, _) in enumerate(preps):
            og = jnp.concatenate(outs[j * heads:(j + 1) * heads], axis=-1)
            store_seq(b0 + j, ya, og, new_ha, new_hq, states[j * heads:(j + 1) * heads])
        return carry

    lax.fori_loop(0, bb // unroll, per_group, 0)

    @pl.when(ci == n_chunks - 1)
    def _():
        sout_ref[...] = state[...]
        last = t_real - (n_chunks - 1) * c
        nbufa_ref[...] = (bch_ref[:, last - (ka - 1):last, cw:2 * cw]
                          * bch_ref[:, last - (ka - 1):last, 2 * cw:])
        nbufq_ref[...] = qkv_ref[:, last - (kq - 1):last, :]


def _mixer(proj3, buf_a, buf_q, s0, w_conv_a, w_conv_q, a_log_pad, dt_pad, g_dn, *, bb, c, t_real, heads, hd,
           unroll):
    bsz, tp, _ = proj3.shape
    cw = heads * hd
    n_chunks = pl.cdiv(tp, c)
    rb = min(c, tp)
    assert n_chunks * rb == tp
    ka, kq = w_conv_a.shape[0], w_conv_q.shape[0]
    kern = functools.partial(_mixer_kernel, bb=bb, c=c, heads=heads, hd=hd, t_real=t_real, n_chunks=n_chunks,
                             unroll=unroll)
    return pl.pallas_call(
        kern,
        grid=(bsz // bb, n_chunks),
        in_specs=[
            pl.BlockSpec((bb, rb, 3 * cw), lambda i, j: (i, j, 0)),
            pl.BlockSpec((bb, rb, 3 * cw), lambda i, j: (i, j, 1)),
            pl.BlockSpec((bb, rb, cw), lambda i, j: (i, j, (3 * cw + 3 * cw + 4 * cw) // cw)),
            pl.BlockSpec((bb, rb, LANES), lambda i, j: (i, j, (3 * cw + 3 * cw + 4 * cw + cw) // LANES)),
            pl.BlockSpec((bb, ka - 1, cw), lambda i, j: (i, 0, 0)),
            pl.BlockSpec((bb, kq - 1, 3 * cw), lambda i, j: (i, 0, 0)),
            pl.BlockSpec((bb, heads, hd, hd), lambda i, j: (i, 0, 0, 0)),
            pl.BlockSpec((ka, cw), lambda i, j: (0, 0)),
            pl.BlockSpec((kq, 3 * cw), lambda i, j: (0, 0)),
            pl.BlockSpec((1, LANES), lambda i, j: (0, 0)),
            pl.BlockSpec((1, LANES), lambda i, j: (0, 0)),
            pl.BlockSpec((1, hd), lambda i, j: (0, 0)),
        ],
        out_specs=[
            pl.BlockSpec((bb, rb, cw), lambda i, j: (i, j, 0)),
            pl.BlockSpec((bb, rb, cw), lambda i, j: (i, j, 0)),
            pl.BlockSpec((bb, ka - 1, cw), lambda i, j: (i, 0, 0)),
            pl.BlockSpec((bb, kq - 1, 3 * cw), lambda i, j: (i, 0, 0)),
            pl.BlockSpec((bb, heads, hd, hd), lambda i, j: (i, 0, 0, 0)),
        ],
        out_shape=[
            jax.ShapeDtypeStruct((bsz, tp, cw), F32),
            jax.ShapeDtypeStruct((bsz, tp, cw), F32),
            jax.ShapeDtypeStruct((bsz, ka - 1, cw), F32),
            jax.ShapeDtypeStruct((bsz, kq - 1, 3 * cw), F32),
            jax.ShapeDtypeStruct((bsz, heads, hd, hd), F32),
        ],
        scratch_shapes=[
            pltpu.VMEM((bb, SUBLANES, cw), F32),
            pltpu.VMEM((bb, SUBLANES, 3 * cw), F32),
            pltpu.VMEM((bb, heads, hd, hd), F32),
        ],
        compiler_params=_cparams(("parallel", "arbitrary")),
        name="mixer",
    )(proj3, proj3, proj3, proj3, buf_a, buf_q, s0, w_conv_a, w_conv_q, a_log_pad, dt_pad, g_dn)


def _out_proj_kernel(x_ref, ya_ref, og_ref, ga_ref, gb_ref, woa_ref, wod_ref, wo_ref, h_ref):
    y_a = jnp.dot(ya_ref[...].astype(BF16), woa_ref[...], preferred_element_type=F32)
    y_b = jnp.dot(og_ref[...].astype(BF16), wod_ref[...], preferred_element_type=F32)
    m = _sigmoid(ga_ref[...]) * y_a + _sigmoid(gb_ref[...]) * y_b
    h_ref[...] = x_ref[...] + jnp.dot(m.astype(BF16), wo_ref[...], preferred_element_type=F32)


def _out_proj(x2d, ya, og, proj, w_out_a, w_out_dn, w_o, tm, gate_block):
    n, d = x2d.shape
    cw = ya.shape[1]
    return pl.pallas_call(
        _out_proj_kernel,
        grid=(n // tm,),
        in_specs=[
            pl.BlockSpec((tm, d), lambda i: (i, 0)),
            pl.BlockSpec((tm, cw), lambda i: (i, 0)),
            pl.BlockSpec((tm, cw), lambda i: (i, 0)),
            pl.BlockSpec((tm, d), lambda i: (i, gate_block)),
            pl.BlockSpec((tm, d), lambda i: (i, gate_block + 1)),
            pl.BlockSpec((cw, d), lambda i: (0, 0)),
            pl.BlockSpec((cw, d), lambda i: (0, 0)),
            pl.BlockSpec((d, d), lambda i: (0, 0)),
        ],
        out_specs=pl.BlockSpec((tm, d), lambda i: (i, 0)),
        out_shape=jax.ShapeDtypeStruct((n, d), F32),
        compiler_params=_cparams(("parallel",)),
        name="out_proj",
    )(x2d, ya, og, proj, proj, w_out_a, w_out_dn, w_o)


def _pack_u_kernel(u_ref, o_ref):
    o_ref[...] = pltpu.bitcast(u_ref[...].astype(BF16), jnp.uint32)


def _pack_vt_kernel(v_ref, o_ref):
    o_ref[0] = pltpu.bitcast(v_ref[...].T.astype(BF16), jnp.uint32)


def _pack_experts(expert_u, expert_v, rows):
    n_exp, d = expert_u.shape
    u_pk = pl.pallas_call(
        _pack_u_kernel,
        grid=(n_exp // rows,),
        in_specs=[pl.BlockSpec((rows, d), lambda i: (i, 0))],
        out_specs=pl.BlockSpec((rows // 2, d), lambda i: (i, 0)),
        out_shape=jax.ShapeDtypeStruct((n_exp // 2, d), jnp.uint32),
        compiler_params=_cparams(("parallel",)),
        name="pack_u",
    )(expert_u)
    vt_pk = pl.pallas_call(
        _pack_vt_kernel,
        grid=(n_exp // rows,),
        in_specs=[pl.BlockSpec((rows, d), lambda i: (i, 0))],
        out_specs=pl.BlockSpec((1, d // 2, rows), lambda i: (i, 0, 0)),
        out_shape=jax.ShapeDtypeStruct((n_exp // rows, d // 2, rows), jnp.uint32),
        compiler_params=_cparams(("parallel",)),
        name="pack_vt",
    )(expert_v)
    return u_pk, vt_pk


def _extract_topk(ss, flat, topk, exact_ties, with_rank=None):
    n, t = ss[0].shape
    big = jnp.int32(2 ** 30)
    ro = lax.broadcasted_iota(jnp.int32, (topk, t), 0)
    with_rank = with_rank or [True] * len(ss)
    cur = list(ss)
    ranks = [jnp.full((n, t), float(topk), F32) if wr else None for wr in with_rank]
    vals = [jnp.zeros((topk, t), F32) for _ in ss]
    for r in range(topk):
        for i in range(len(ss)):
            m = jnp.max(cur[i], axis=0, keepdims=True)
            hit = cur[i] == m
            if exact_ties:
                hit = flat == jnp.min(jnp.where(hit, flat, big), axis=0, keepdims=True)
            if with_rank[i]:
                ranks[i] = jnp.where(hit, float(r), ranks[i])
            cur[i] = jnp.where(hit, NEG_INF, cur[i])
            vals[i] = jnp.where(ro == r, m, vals[i])
    return vals, ranks, cur


def _tie_count(marks, topk):
    bad = None
    for mk in marks:
        n_marked = jnp.sum(jnp.where(mk, 1.0, 0.0), axis=0, keepdims=True)
        bad = jnp.abs(n_marked - topk) if bad is None else bad + jnp.abs(n_marked - topk)
    return bad


def _pair_candidates(v1, v2, topk):
    io8 = lax.broadcasted_iota(jnp.int32, (SUBLANES, 1), 0)
    pieces = [v1[0:1] + v2]
    flats = [lax.broadcasted_iota(jnp.int32, (topk, 1), 0)]
    for a in range(1, SUBLANES):
        pieces.append(jnp.where(io8 < (topk // (a + 1)), v1[a:a + 1] + v2[0:SUBLANES], NEG_INF))
        flats.append(a * topk + io8)
    pieces.append(v1[SUBLANES:] + v2[0:1])
    flats.append((SUBLANES + io8) * topk)
    return jnp.concatenate(pieces, axis=0), jnp.concatenate(flats, axis=0)


def _pair_counts(sums, rank, mtot, topk):
    t = sums.shape[1]
    sel = jnp.where(rank < topk, 1.0, 0.0)
    z = jnp.sum(sel * jnp.exp(jnp.where(rank < topk, sums, mtot) - mtot), axis=0, keepdims=True)
    ro = lax.broadcasted_iota(jnp.int32, (topk, t), 0)
    nb = jnp.zeros((topk, t), F32)
    nb = jnp.where(ro == 0, jnp.sum(sel[0:topk], axis=0, keepdims=True), nb)
    for a in range(1, SUBLANES):
        lo = topk + (a - 1) * SUBLANES
        nb = jnp.where(ro == a, jnp.sum(sel[lo:lo + SUBLANES], axis=0, keepdims=True), nb)
    tail = sel[topk + (SUBLANES - 1) * SUBLANES:]
    nb = jnp.where(ro >= SUBLANES, jnp.concatenate([tail, tail], axis=0), nb)
    return nb, z


def _peer_select_kernel(h_ref, g_ref, wqt_ref, keys_ref, xn_ref, r2_ref, e2_ref, cnt_ref, c1_ref, q_scr,
                        *, heads, half, topk, group):
    xf = _rmsnorm(h_ref[...], g_ref[...])
    xn = xf.astype(BF16)
    xn_ref[...] = pltpu.bitcast(xf.T.astype(BF16), jnp.uint32)
    q_scr[...] = lax.dot_general(wqt_ref[...], xn, (((1,), (1,)), ((), ())), preferred_element_type=F32)
    nkeys = keys_ref.shape[1]
    key_idx = lax.broadcasted_iota(jnp.int32, (nkeys, 1), 0)

    def select_heads(exact_ties):
        def per_group(gi, bad):
            hs = [gi * group + j for j in range(group)]
            scores = []
            for h in hs:
                base = pl.multiple_of(h * 2 * half, 2 * half)
                scores.append(_mm(keys_ref[h], q_scr[pl.ds(base, half), :]))
                scores.append(_mm(keys_ref[heads + h], q_scr[pl.ds(base + half, half), :]))
            with_rank = [exact_ties or k % 2 == 1 for k in range(2 * group)]
            vals, ranks, left = _extract_topk(scores, key_idx, topk, exact_ties, with_rank)
            cands = [_pair_candidates(vals[2 * j], vals[2 * j + 1], topk) for j in range(group)]
            _, pair_ranks, _ = _extract_topk([cd[0] for cd in cands], cands[0][1], topk, exact_ties)
            for j, h in enumerate(hs):
                v1, r1, v2, r2 = vals[2 * j], ranks[2 * j], vals[2 * j + 1], ranks[2 * j + 1]
                nb, z = _pair_counts(cands[j][0], pair_ranks[j], v1[0:1] + v2[0:1], topk)
                cnt = jnp.zeros(scores[2 * j].shape, F32)
                for a in range(topk):
                    is_a = (r1 == float(a)) if r1 is not None else (scores[2 * j] == v1[a:a + 1])
                    cnt = jnp.where(is_a, nb[a:a + 1], cnt)
                r2_ref[h, 0] = pltpu.bitcast(r2.astype(BF16), jnp.uint32)
                e2_ref[h, 0] = pltpu.bitcast(jnp.exp(scores[2 * j + 1] - v2[0:1]).astype(BF16), jnp.uint32)
                cnt_ref[h, 0] = cnt
                c1_ref[h, 0] = jnp.exp(scores[2 * j] - v1[0:1]) / z
            marks = [lf == NEG_INF for lf in left] + [rk < topk for rk in pair_ranks]
            return bad + _tie_count(marks, topk)

        return lax.fori_loop(0, heads // group, per_group, jnp.zeros((1, h_ref.shape[0]), F32))

    ties = select_heads(exact_ties=False)

    @pl.when(jnp.max(ties) > 0.0)
    def _():
        select_heads(exact_ties=True)


def _peer_select(h2d, g, wq_t, keys, *, heads, nkeys, half, topk, tms):
    n, d = h2d.shape
    qd = wq_t.shape[0]
    kern = functools.partial(_peer_select_kernel, heads=heads, half=half, topk=topk, group=4)
    assert tms == LANES
    sel_f32 = jax.ShapeDtypeStruct((heads, n // LANES, nkeys, LANES), F32)
    sel_pk = jax.ShapeDtypeStruct((heads, n // LANES, nkeys // 2, LANES), jnp.uint32)
    sel_spec = pl.BlockSpec((heads, 1, nkeys, LANES), lambda i: (0, i, 0, 0))
    pk_spec = pl.BlockSpec((heads, 1, nkeys // 2, LANES), lambda i: (0, i, 0, 0))
    return pl.pallas_call(
        kern,
        grid=(n // tms,),
        in_specs=[
            pl.BlockSpec((tms, d), lambda i: (i, 0)),
            pl.BlockSpec((1, d), lambda i: (0, 0)),
            pl.BlockSpec((qd, d), lambda i: (0, 0)),
            pl.BlockSpec(keys.shape, lambda i: (0, 0, 0)),
        ],
        out_specs=[pl.BlockSpec((d // 2, tms), lambda i: (0, i)), pk_spec, pk_spec, sel_spec, sel_spec],
        out_shape=[jax.ShapeDtypeStruct((d // 2, n), jnp.uint32), sel_pk, sel_pk, sel_f32, sel_f32],
        scratch_shapes=[pltpu.VMEM((qd, tms), F32)],
        compiler_params=_cparams(("parallel",)),
        name="peer_select",
    )(h2d, g, wq_t, keys)


def _pair_index(s, offset, n_pairs):
    return jnp.clip(s - offset, 0, n_pairs - 1)


def _peer_dense_kernel(xn_ref, u_ref, vt_ref, r2_ref, e2_ref, cnt_ref, c1_ref, h_ref, g_ref, y_ref,
                       acc_ref, ht0_ref, ht1_ref, w0_ref, w1_ref, *, heads, nkeys, eb, tm, n_eb, n_pairs):
    s = pl.program_id(0)
    eb_b = lax.rem(_pair_index(s, 1, n_pairs), n_eb)
    eb_c = lax.rem(_pair_index(s, 2, n_pairs), n_eb)

    @pl.when(s == 0)
    def _():
        ht1_ref[...] = jnp.zeros(ht1_ref.shape, F32)
        w0_ref[...] = jnp.zeros(w0_ref.shape, jnp.uint32)
        w1_ref[...] = jnp.zeros(w1_ref.shape, jnp.uint32)

    @pl.when(eb_c == 0)
    def _():
        acc_ref[...] = jnp.zeros(acc_ref.shape, F32)

    i0 = pl.multiple_of(eb_b * (eb // nkeys), SUBLANES)
    zero = jnp.zeros((nkeys, LANES), BF16)

    def step(ht_cur, ht_prev, w_cur, w_prev):
        d = acc_ref.shape[0]
        xn_t = pltpu.bitcast(xn_ref[...], BF16)
        w_in = jnp.concatenate([pltpu.bitcast(w_prev[lc], BF16) for lc in range(tm // LANES)], axis=1)
        for q in range(MM_PIECES):
            dr = slice(q * (d // MM_PIECES), (q + 1) * (d // MM_PIECES))
            dr2 = slice(q * (d // 2 // MM_PIECES), (q + 1) * (d // 2 // MM_PIECES))
            acc_ref[dr, :] += jnp.dot(pltpu.bitcast(vt_ref[0, dr2, :], BF16), w_in, preferred_element_type=F32)
            rows = eb // nkeys // MM_PIECES
            for ii in range(q * rows, (q + 1) * rows):
                rs = slice(ii * nkeys, (ii + 1) * nkeys)
                rs2 = slice(ii * (nkeys // 2), (ii + 1) * (nkeys // 2))
                for lc in range(tm // LANES):
                    gate = zero
                    for h in range(heads):
                        cnt = cnt_ref[h, lc, pl.ds(i0, SUBLANES), :][ii:ii + 1]
                        c1 = c1_ref[h, lc, pl.ds(i0, SUBLANES), :][ii:ii + 1]
                        cnt = jnp.broadcast_to(cnt, (nkeys, LANES)).astype(BF16)
                        c1 = jnp.broadcast_to(c1, (nkeys, LANES)).astype(BF16)
                        r2 = pltpu.bitcast(r2_ref[h, lc], BF16)
                        e2 = pltpu.bitcast(e2_ref[h, lc], BF16)
                        gate = gate + jnp.where(r2 < cnt, e2, zero) * c1
                    act = _gelu_exact(ht_prev[lc, rs, :]).astype(BF16)
                    w_cur[lc, rs2, :] = pltpu.bitcast(gate * act, jnp.uint32)
            er = slice(q * (eb // MM_PIECES), (q + 1) * (eb // MM_PIECES))
            er2 = slice(q * (eb // 2 // MM_PIECES), (q + 1) * (eb // 2 // MM_PIECES))
            res = jnp.dot(pltpu.bitcast(u_ref[er2, :], BF16), xn_t, preferred_element_type=F32)
            for lc in range(tm // LANES):
                ht_cur[lc, er, :] = res[:, lc * LANES:(lc + 1) * LANES]

    parity = lax.rem(s, 2)

    @pl.when(parity == 0)
    def _():
        step(ht0_ref, ht1_ref, w0_ref, w1_ref)

    @pl.when(parity == 1)
    def _():
        step(ht1_ref, ht0_ref, w1_ref, w0_ref)

    @pl.when(jnp.logical_and(s >= 2, eb_c == n_eb - 1))
    def _():
        y_ref[...] = _rmsnorm(h_ref[...] + acc_ref[...].T, g_ref[...])


def _peer_dense(xn, u, vt, r2, e2, cnt, c1, h2d, g, *, heads, nkeys, eb, tm):
    n, d = h2d.shape
    n_exp = 2 * u.shape[0]
    assert eb == SUBLANES * nkeys
    assert vt.shape == (n_exp // eb, d // 2, eb)
    n_eb = n_exp // eb
    n_pairs = (n // tm) * n_eb
    kern = functools.partial(_peer_dense_kernel, heads=heads, nkeys=nkeys, eb=eb, tm=tm, n_eb=n_eb, n_pairs=n_pairs)

    def tb(offset):
        return lambda s: _pair_index(s, offset, n_pairs) // n_eb

    def ebk(offset):
        return lambda s: lax.rem(_pair_index(s, offset, n_pairs), n_eb)

    sel_spec = pl.BlockSpec((heads, tm // LANES, nkeys, LANES), lambda s: (0, tb(1)(s), 0, 0))
    pk_spec = pl.BlockSpec((heads, tm // LANES, nkeys // 2, LANES), lambda s: (0, tb(1)(s), 0, 0))
    return pl.pallas_call(
        kern,
        grid=(n_pairs + 2,),
        in_specs=[
            pl.BlockSpec((d // 2, tm), lambda s: (0, tb(0)(s))),
            pl.BlockSpec((eb // 2, d), lambda s: (ebk(0)(s), 0)),
            pl.BlockSpec((1, d // 2, eb), lambda s: (ebk(2)(s), 0, 0)),
            pk_spec, pk_spec, sel_spec, sel_spec,
            pl.BlockSpec((tm, d), lambda s: (tb(2)(s), 0)),
            pl.BlockSpec((1, d), lambda s: (0, 0)),
        ],
        out_specs=pl.BlockSpec((tm, d), lambda s: (tb(2)(s), 0)),
        out_shape=jax.ShapeDtypeStruct((n, d), F32),
        scratch_shapes=[
            pltpu.VMEM((d, tm), F32),
            pltpu.VMEM((tm // LANES, eb, LANES), F32),
            pltpu.VMEM((tm // LANES, eb, LANES), F32),
            pltpu.VMEM((tm // LANES, eb // 2, LANES), jnp.uint32),
            pltpu.VMEM((tm // LANES, eb // 2, LANES), jnp.uint32),
        ],
        compiler_params=_cparams(("arbitrary",)),
        name="peer_dense",
    )(xn, u, vt, r2, e2, cnt, c1, h2d, g)


def _pick_block(n, candidates):
    for c in candidates:
        if n % c == 0:
            return c
    raise ValueError(f"no block size in {candidates} divides {n}")


def _trunk(x, buf_a, buf_q, s0, wts, dims):
    heads, hd, cw, p_heads, nkeys, half = dims
    (g_mix, w_in, w_conv_a, w_conv_q, a_log_pad, dt_pad, g_dn, w_out_a, w_out_dn, w_o, g_ffn,
     wq_t, keys, u_pk, vt_pk, g_final) = wts
    bsz, tp, d = x.shape
    n = bsz * tp
    x2d = x.reshape(n, d)
    c = DN_CHUNK if tp % DN_CHUNK == 0 else SUBLANES
    ka, kq = w_conv_a.shape[0], w_conv_q.shape[0]
    last = tp - (pl.cdiv(tp, c) - 1) * c
    assert (tp % c == 0 or tp < c) and last >= kq - 1 and last >= ka - 1, (tp, c)

    proj = _in_proj(x2d, g_mix, w_in, _pick_block(n, (256, 128)))
    if c == DN_CHUNK:
        bb, unroll = _pick_block(bsz, (4, 2, 1)), 2
    else:
        bb, unroll = _pick_block(bsz, (16, 8, 4, 2, 1)), 16
    ya, og, nbuf_a, nbuf_q, s_new = _mixer(
        proj.reshape(bsz, tp, -1), buf_a, buf_q, s0, w_conv_a, w_conv_q, a_log_pad, dt_pad, g_dn,
        bb=bb, c=c, t_real=tp, heads=heads, hd=hd, unroll=min(unroll, bb))
    tm2 = _pick_block(n, (512, 256, 128))
    hres = _out_proj(x2d, ya.reshape(n, cw), og.reshape(n, cw), proj, w_out_a, w_out_dn, w_o, tm2,
                     gate_block=(6 * cw) // d)
    xn_t, r2, e2, cnt, c1 = _peer_select(hres, g_ffn, wq_t, keys, heads=p_heads, nkeys=nkeys, half=half,
                                         topk=PEER_TOPK, tms=LANES)
    tm3 = _pick_block(n, (512, 256, 128))
    y = _peer_dense(xn_t, u_pk, vt_pk, r2, e2, cnt, c1, hres, g_final, heads=p_heads, nkeys=nkeys,
                    eb=SUBLANES * nkeys, tm=tm3)
    return y.reshape(bsz, tp, d), nbuf_a, nbuf_q, s_new


def kernel(x_prompt, x_sample, state_conv_a, state_conv_qkv, state_delta, g_norm_mix, w_in, w_conv_a, w_conv_qkv,
           a_log, dt_bias, g_dn_norm, w_out_a, w_out_dn, w_o, g_norm_ffn, w_query, sub_keys, expert_u, expert_v,
           g_norm_final):
    depth = w_in.shape[0]
    assert depth == 1
    d = x_prompt.shape[-1]
    heads = a_log.shape[1]
    hd = g_dn_norm.shape[1]
    cw = heads * hd
    assert w_conv_a.shape[2] == cw and w_conv_qkv.shape[2] == 3 * cw and d == 2 * cw
    p_heads, nkeys, half = sub_keys.shape[2], sub_keys.shape[3], sub_keys.shape[4]
    assert nkeys == LANES and expert_u.shape[1] == nkeys * nkeys

    w = w_in[0]
    o_qkv, o_z, o_a, o_b, o_g = 3 * cw, 6 * cw, 7 * cw, 7 * cw + heads, 7 * cw + 2 * heads
    ab_pad = jnp.zeros((d, LANES - 2 * heads), w.dtype)
    w_r = jnp.concatenate([w[:, :o_z], w[:, o_g:], w[:, o_z:o_a], w[:, o_a:o_g], ab_pad], axis=1).astype(BF16)
    pad4 = lambda v: jnp.pad(v.astype(F32), ((0, 0), (0, LANES - v.shape[1])))
    wts = (
        g_norm_mix, w_r, w_conv_a[0], w_conv_qkv[0], pad4(a_log), pad4(dt_bias), g_dn_norm,
        w_out_a[0].astype(BF16), w_out_dn[0].astype(BF16), w_o[0].astype(BF16), g_norm_ffn,
        w_query[0].T.astype(BF16), sub_keys[0].reshape(2 * p_heads, nkeys, half).astype(BF16),
        *_pack_experts(expert_u[0], expert_v[0], SUBLANES * nkeys), g_norm_final.reshape(1, d),
    )
    dims = (heads, hd, cw, p_heads, nkeys, half)

    bp, tpr, _ = x_prompt.shape
    assert tpr % DN_CHUNK == 0
    ka, kq = w_conv_a.shape[1], w_conv_qkv.shape[1]
    zero_a = jnp.zeros((bp, ka - 1, cw), F32)
    zero_q = jnp.zeros((bp, kq - 1, 3 * cw), F32)
    zero_s = jnp.zeros((bp, heads, hd, hd), F32)
    y_p, a_p, q_p, s_p = _trunk(x_prompt, zero_a, zero_q, zero_s, wts, dims)

    assert x_sample.shape[1] <= SUBLANES
    y_s, a_s, q_s, s_s = _trunk(x_sample, state_conv_a[0], state_conv_qkv[0], state_delta[0], wts, dims)
    return (y_p, y_s, a_p[None], q_p[None], s_p[None], a_s[None], q_s[None], s_s[None])
```

```python
import functools

import jax
import jax.numpy as jnp
from jax import lax
from jax.experimental import pallas as pl
from jax.experimental.pallas import tpu as pltpu

EPS = 1e-6
F32 = jnp.float32
BF16 = jnp.bfloat16
LANES = 128
SUBLANES = 8
VMEM_LIMIT = 56 * 1024 * 1024

PEER_TOPK = 16
DN_CHUNK = 128
MM_PIECES = 8
SUB_BLOCK = 1024
EXPERT_BLOCK = 2048
NEG_INF = float("-inf")


def _cparams(sem):
    return pltpu.CompilerParams(dimension_semantics=sem, vmem_limit_bytes=VMEM_LIMIT)


def _mm(a, b):
    return jnp.dot(a.astype(BF16), b.astype(BF16), preferred_element_type=F32)


def _mm_nt(a, b):
    return lax.dot_general(a.astype(BF16), b.astype(BF16), (((1,), (1,)), ((), ())), preferred_element_type=F32)


def _mm_tn(a, b):
    return lax.dot_general(a.astype(BF16), b.astype(BF16), (((0,), (0,)), ((), ())), preferred_element_type=F32)


def _rmsnorm(x, g):
    return x * lax.rsqrt(jnp.mean(x * x, axis=-1, keepdims=True) + EPS) * g


def _sigmoid(x):
    return 1.0 / (1.0 + jnp.exp(-x))


def _silu(x):
    return x * _sigmoid(x)


def _softplus(x):
    return jnp.maximum(x, 0.0) + jnp.log(1.0 + jnp.exp(-jnp.abs(x)))


def _gelu_exact(x):
    return 0.5 * x * (1.0 + lax.erf(x * (2.0 ** -0.5)))


def _in_proj_kernel(x_ref, g_ref, w_ref, o_ref):
    xn = _rmsnorm(x_ref[...], g_ref[...]).astype(BF16)
    o_ref[...] = jnp.dot(xn, w_ref[...], preferred_element_type=F32)


def _in_proj(x2d, g, w, tm):
    n, d = x2d.shape
    width = w.shape[1]
    return pl.pallas_call(
        _in_proj_kernel,
        grid=(n // tm,),
        in_specs=[
            pl.BlockSpec((tm, d), lambda i: (i, 0)),
            pl.BlockSpec((1, d), lambda i: (0, 0)),
            pl.BlockSpec((d, width), lambda i: (0, 0)),
        ],
        out_specs=pl.BlockSpec((tm, width), lambda i: (i, 0)),
        out_shape=jax.ShapeDtypeStruct((n, width), F32),
        compiler_params=_cparams(("parallel",)),
        name="in_proj",
    )(x2d, g, w)


def _shift_rows(xp, k, rows):
    if k == 0:
        return xp[SUBLANES:SUBLANES + rows]
    return pltpu.roll(xp, k, axis=0)[SUBLANES:SUBLANES + rows]


def _cumsum_rows(x, rows):
    ridx = lax.broadcasted_iota(jnp.int32, x.shape, 0)
    s = 1
    while s < rows:
        x = x + jnp.where(ridx >= s, pltpu.roll(x, s, axis=0), 0.0)
        s *= 2
    return x


def _unit_lower_inverse_many(ms, c):
    eye = (lax.broadcasted_iota(jnp.int32, (c, c), 0) == lax.broadcasted_iota(jnp.int32, (c, c), 1)).astype(F32)
    n_factors = c.bit_length() - 1
    assert 1 << n_factors == c
    ps = [-m for m in ms]
    ts = [eye + p for p in ps]
    if n_factors == 1:
        return ts
    ps = [_mm(p, p) for p in ps]
    for _ in range(n_factors - 2):
        tps = [_mm(jnp.concatenate([t, p], axis=0), p) for t, p in zip(ts, ps)]
        ts = [t + tp[:c] for t, tp in zip(ts, tps)]
        ps = [tp[c:] for tp in tps]
    return [t + _mm(t, p) for t, p in zip(ts, ps)]


def _mixer_kernel(bch_ref, qkv_ref, z_ref, ab_ref, bufa_ref, bufq_ref, s0_ref,
                  wca_ref, wcq_ref, alog_ref, dtb_ref, gdn_ref,
                  ya_ref, og_ref, nbufa_ref, nbufq_ref, sout_ref,
                  halo_a, halo_q, state,
                  *, bb, c, heads, hd, t_real, n_chunks, unroll):
    ci = pl.program_id(1)
    cw = heads * hd
    ka = wca_ref.shape[0]
    kq = wcq_ref.shape[0]

    @pl.when(ci == 0)
    def _():
        state[...] = s0_ref[...]
        halo_a[...] = jnp.zeros(halo_a.shape, F32)
        halo_q[...] = jnp.zeros(halo_q.shape, F32)
        halo_a[:, SUBLANES - (ka - 1):, :] = bufa_ref[...]
        halo_q[:, SUBLANES - (kq - 1):, :] = bufq_ref[...]

    wca = wca_ref[...]
    wcq = wcq_ref[...]
    neg_a = -jnp.exp(alog_ref[...])
    dtb = dtb_ref[...]
    gdn = gdn_ref[...]
    rows_left = t_real - ci * c
    ridx = lax.broadcasted_iota(jnp.int32, (c, 1), 0)
    valid = ridx < rows_left
    ri = lax.broadcasted_iota(jnp.int32, (c, c), 0)
    si = lax.broadcasted_iota(jnp.int32, (c, c), 1)
    causal = ri >= si
    strict = ri > si
    cpad = max(c, LANES)

    rb = bch_ref.shape[1]

    def pad_rows(x):
        if rb == c:
            return x
        return jnp.concatenate([x, jnp.zeros((c - rb, x.shape[1]), x.dtype)], axis=0)

    def load_seq(b):
        return (pad_rows(bch_ref[b]), pad_rows(qkv_ref[b]), pad_rows(z_ref[b]), pad_rows(ab_ref[b]),
                halo_a[b], halo_q[b], [state[b, h] for h in range(heads)])

    def prep_seq(bch, qkv, z, ab, ha, hq, s_prev_all):
        b_c, c_c, h_c = bch[:, :cw], bch[:, cw:2 * cw], bch[:, 2 * cw:]
        ta = c_c * h_c
        xpa = jnp.concatenate([ha, ta], axis=0)
        conv_a = wca[ka - 1:ka] * _shift_rows(xpa, 0, c)
        for j in range(1, ka):
            conv_a = conv_a + wca[ka - 1 - j:ka - j] * _shift_rows(xpa, j, c)
        ya = b_c * conv_a
        xpq = jnp.concatenate([hq, qkv], axis=0)
        conv_q = wcq[kq - 1:kq] * _shift_rows(xpq, 0, c)
        for j in range(1, kq):
            conv_q = conv_q + wcq[kq - 1 - j:kq - j] * _shift_rows(xpq, j, c)
        conv_q = _silu(conv_q)
        new_ha, new_hq = xpa[c:c + SUBLANES], xpq[c:c + SUBLANES]
        g_all = jnp.where(valid, neg_a * _softplus(ab + dtb), 0.0)
        beta_all = jnp.where(valid, _sigmoid(ab), 0.0)
        gcum = _cumsum_rows(g_all, c)
        if c < cpad:
            gpad = jnp.concatenate([gcum, jnp.zeros((cpad - c, LANES), F32)], axis=0)
        else:
            gpad = gcum
        gcum_t = gpad.T
        chains = []
        for h in range(heads):
            q = conv_q[:, h * hd:(h + 1) * hd]
            k = conv_q[:, cw + h * hd:cw + (h + 1) * hd]
            v = conv_q[:, 2 * cw + h * hd:2 * cw + (h + 1) * hd]
            q = q * lax.rsqrt(jnp.sum(q * q, axis=-1, keepdims=True) + EPS) * (hd ** -0.5)
            k = k * lax.rsqrt(jnp.sum(k * k, axis=-1, keepdims=True) + EPS)
            q = jnp.where(valid, q, 0.0)
            k = jnp.where(valid, k, 0.0)
            v = jnp.where(valid, v, 0.0)
            gc = gcum[:, h:h + 1]
            gr = gcum_t[h:h + 1, :c]
            beta = beta_all[:, heads + h:heads + h + 1]
            g_last = gcum[c - 1:c, h:h + 1]
            eg = jnp.exp(gc)
            chains.append(dict(
                q=q, k=k, beta=beta, g_last=g_last, s_prev=s_prev_all[h], z=z[:, h * hd:(h + 1) * hd],
                decay=jnp.exp(jnp.where(causal, gc - gr, NEG_INF)),
                rhs=jnp.concatenate([v * beta, k * (beta * eg)], axis=-1),
                q_dec=q * eg, k_dec=k * jnp.exp(g_last - gc)))
        return ya, new_ha, new_hq, chains

    def delta_chains(ch):
        kkqk = [_mm_nt(jnp.concatenate([x["k"], x["q"]], axis=0), x["k"]) for x in ch]
        ms = [jnp.where(strict, kq_[:c] * x["decay"] * x["beta"], 0.0) for kq_, x in zip(kkqk, ch)]
        qks = [kq_[c:] * x["decay"] for kq_, x in zip(kkqk, ch)]
        t_invs = _unit_lower_inverse_many(ms, c)
        sols = [_mm(t, x["rhs"]) for t, x in zip(t_invs, ch)]
        ws_qs = [_mm(jnp.concatenate([sol[:, hd:], x["q_dec"]], axis=0), x["s_prev"])
                 for sol, x in zip(sols, ch)]
        v_news = [sol[:, :hd] - wq[:c] for sol, wq in zip(sols, ws_qs)]
        if c >= LANES:
            rs = [_mm(jnp.concatenate([qk, x["k_dec"].T], axis=0), vn)
                  for qk, x, vn in zip(qks, ch, v_news)]
            os_ = [wq[c:] + r[:c] for wq, r in zip(ws_qs, rs)]
            upd = [r[c:] for r in rs]
        else:
            os_ = [wq[c:] + _mm(qk, vn) for wq, qk, vn in zip(ws_qs, qks, v_news)]
            upd = [_mm_tn(x["k_dec"], vn) for x, vn in zip(ch, v_news)]
        states = [x["s_prev"] * jnp.exp(x["g_last"]) + u_ for x, u_ in zip(ch, upd)]
        outs = [o * lax.rsqrt(jnp.mean(o * o, axis=-1, keepdims=True) + EPS) * gdn * _silu(x["z"])
                for o, x in zip(os_, ch)]
        return outs, states

    def store_seq(b, ya, og, new_ha, new_hq, new_states):
        ya_ref[b] = ya[:rb]
        og_ref[b] = og[:rb]
        halo_a[b] = new_ha
        halo_q[b] = new_hq
        for h in range(heads):
            state[b, h] = new_states[h]

    def per_group(gi, carry):
        b0 = gi * unroll
        preps = [prep_seq(*load_seq(b0 + j)) for j in range(unroll)]
        outs, states = delta_chains([x for pr in preps for x in pr[3]])
        for j, (ya, new_ha, new_hq, _) in enumerate(preps):
            og = jnp.concatenate(outs[j * heads:(j + 1) * heads], axis=-1)
            store_seq(b0 + j, ya, og, new_ha, new_hq, states[j * heads:(j + 1) * heads])
        return carry

    lax.fori_loop(0, bb // unroll, per_group, 0)

    @pl.when(ci == n_chunks - 1)
    def _():
        sout_ref[...] = state[...]
        last = t_real - (n_chunks - 1) * c
        nbufa_ref[...] = (bch_ref[:, last - (ka - 1):last, cw:2 * cw]
                          * bch_ref[:, last - (ka - 1):last, 2 * cw:])
        nbufq_ref[...] = qkv_ref[:, last - (kq - 1):last, :]


def _mixer(proj3, buf_a, buf_q, s0, w_conv_a, w_conv_q, a_log_pad, dt_pad, g_dn, *, bb, c, t_real, heads, hd,
           unroll):
    bsz, tp, _ = proj3.shape
    cw = heads * hd
    n_chunks = pl.cdiv(tp, c)
    rb = min(c, tp)
    assert n_chunks * rb == tp
    ka, kq = w_conv_a.shape[0], w_conv_q.shape[0]
    kern = functools.partial(_mixer_kernel, bb=bb, c=c, heads=heads, hd=hd, t_real=t_real, n_chunks=n_chunks,
                             unroll=unroll)
    return pl.pallas_call(
        kern,
        grid=(bsz // bb, n_chunks),
        in_specs=[
            pl.BlockSpec((bb, rb, 3 * cw), lambda i, j: (i, j, 0)),
            pl.BlockSpec((bb, rb, 3 * cw), lambda i, j: (i, j, 1)),
            pl.BlockSpec((bb, rb, cw), lambda i, j: (i, j, (3 * cw + 3 * cw + 4 * cw) // cw)),
            pl.BlockSpec((bb, rb, LANES), lambda i, j: (i, j, (3 * cw + 3 * cw + 4 * cw + cw) // LANES)),
            pl.BlockSpec((bb, ka - 1, cw), lambda i, j: (i, 0, 0)),
            pl.BlockSpec((bb, kq - 1, 3 * cw), lambda i, j: (i, 0, 0)),
            pl.BlockSpec((bb, heads, hd, hd), lambda i, j: (i, 0, 0, 0)),
            pl.BlockSpec((ka, cw), lambda i, j: (0, 0)),
            pl.BlockSpec((kq, 3 * cw), lambda i, j: (0, 0)),
            pl.BlockSpec((1, LANES), lambda i, j: (0, 0)),
            pl.BlockSpec((1, LANES), lambda i, j: (0, 0)),
            pl.BlockSpec((1, hd), lambda i, j: (0, 0)),
        ],
        out_specs=[
            pl.BlockSpec((bb, rb, cw), lambda i, j: (i, j, 0)),
            pl.BlockSpec((bb, rb, cw), lambda i, j: (i, j, 0)),
            pl.BlockSpec((bb, ka - 1, cw), lambda i, j: (i, 0, 0)),
            pl.BlockSpec((bb, kq - 1, 3 * cw), lambda i, j: (i, 0, 0)),
            pl.BlockSpec((bb, heads, hd, hd), lambda i, j: (i, 0, 0, 0)),
        ],
        out_shape=[
            jax.ShapeDtypeStruct((bsz, tp, cw), F32),
            jax.ShapeDtypeStruct((bsz, tp, cw), F32),
            jax.ShapeDtypeStruct((bsz, ka - 1, cw), F32),
            jax.ShapeDtypeStruct((bsz, kq - 1, 3 * cw), F32),
            jax.ShapeDtypeStruct((bsz, heads, hd, hd), F32),
        ],
        scratch_shapes=[
            pltpu.VMEM((bb, SUBLANES, cw), F32),
            pltpu.VMEM((bb, SUBLANES, 3 * cw), F32),
            pltpu.VMEM((bb, heads, hd, hd), F32),
        ],
        compiler_params=_cparams(("parallel", "arbitrary")),
        name="mixer",
    )(proj3, proj3, proj3, proj3, buf_a, buf_q, s0, w_conv_a, w_conv_q, a_log_pad, dt_pad, g_dn)


def _out_proj_kernel(x_ref, ya_ref, og_ref, ga_ref, gb_ref, woa_ref, wod_ref, wo_ref, h_ref):
    y_a = jnp.dot(ya_ref[...].astype(BF16), woa_ref[...], preferred_element_type=F32)
    y_b = jnp.dot(og_ref[...].astype(BF16), wod_ref[...], preferred_element_type=F32)
    m = _sigmoid(ga_ref[...]) * y_a + _sigmoid(gb_ref[...]) * y_b
    h_ref[...] = x_ref[...] + jnp.dot(m.astype(BF16), wo_ref[...], preferred_element_type=F32)


def _out_proj(x2d, ya, og, proj, w_out_a, w_out_dn, w_o, tm, gate_block):
    n, d = x2d.shape
    cw = ya.shape[1]
    return pl.pallas_call(
        _out_proj_kernel,
        grid=(n // tm,),
        in_specs=[
            pl.BlockSpec((tm, d), lambda i: (i, 0)),
            pl.BlockSpec((tm, cw), lambda i: (i, 0)),
            pl.BlockSpec((tm, cw), lambda i: (i, 0)),
            pl.BlockSpec((tm, d), lambda i: (i, gate_block)),
            pl.BlockSpec((tm, d), lambda i: (i, gate_block + 1)),
            pl.BlockSpec((cw, d), lambda i: (0, 0)),
            pl.BlockSpec((cw, d), lambda i: (0, 0)),
            pl.BlockSpec((d, d), lambda i: (0, 0)),
        ],
        out_specs=pl.BlockSpec((tm, d), lambda i: (i, 0)),
        out_shape=jax.ShapeDtypeStruct((n, d), F32),
        compiler_params=_cparams(("parallel",)),
        name="out_proj",
    )(x2d, ya, og, proj, proj, w_out_a, w_out_dn, w_o)


def _pack_u_kernel(u_ref, o_ref):
    o_ref[...] = pltpu.bitcast(u_ref[...].astype(BF16), jnp.uint32)


def _pack_vt_kernel(v_ref, o_ref):
    o_ref[0] = pltpu.bitcast(v_ref[...].T.astype(BF16), jnp.uint32)


def _pack_experts(expert_u, expert_v, rows):
    n_exp, d = expert_u.shape
    u_pk = pl.pallas_call(
        _pack_u_kernel,
        grid=(n_exp // rows,),
        in_specs=[pl.BlockSpec((rows, d), lambda i: (i, 0))],
        out_specs=pl.BlockSpec((rows // 2, d), lambda i: (i, 0)),
        out_shape=jax.ShapeDtypeStruct((n_exp // 2, d), jnp.uint32),
        compiler_params=_cparams(("parallel",)),
        name="pack_u",
    )(expert_u)
    vt_pk = pl.pallas_call(
        _pack_vt_kernel,
        grid=(n_exp // rows,),
        in_specs=[pl.BlockSpec((rows, d), lambda i: (i, 0))],
        out_specs=pl.BlockSpec((1, d // 2, rows), lambda i: (i, 0, 0)),
        out_shape=jax.ShapeDtypeStruct((n_exp // rows, d // 2, rows), jnp.uint32),
        compiler_params=_cparams(("parallel",)),
        name="pack_vt",
    )(expert_v)
    return u_pk, vt_pk


def _extract_topk(ss, flat, topk, exact_ties, with_rank=None):
    n, t = ss[0].shape
    big = jnp.int32(2 ** 30)
    ro = lax.broadcasted_iota(jnp.int32, (topk, t), 0)
    with_rank = with_rank or [True] * len(ss)
    cur = list(ss)
    ranks = [jnp.full((n, t), float(topk), F32) if wr else None for wr in with_rank]
    vals = [jnp.zeros((topk, t), F32) for _ in ss]
    for r in range(topk):
        for i in range(len(ss)):
            m = jnp.max(cur[i], axis=0, keepdims=True)
            hit = cur[i] == m
            if exact_ties:
                hit = flat == jnp.min(jnp.where(hit, flat, big), axis=0, keepdims=True)
            if with_rank[i]:
                ranks[i] = jnp.where(hit, float(r), ranks[i])
            cur[i] = jnp.where(hit, NEG_INF, cur[i])
            vals[i] = jnp.where(ro == r, m, vals[i])
    return vals, ranks, cur


def _tie_count(marks, topk):
    bad = None
    for mk in marks:
        n_marked = jnp.sum(jnp.where(mk, 1.0, 0.0), axis=0, keepdims=True)
        bad = jnp.abs(n_marked - topk) if bad is None else bad + jnp.abs(n_marked - topk)
    return bad


def _pair_candidates(v1, v2, topk):
    io8 = lax.broadcasted_iota(jnp.int32, (SUBLANES, 1), 0)
    pieces = [v1[0:1] + v2]
    flats = [lax.broadcasted_iota(jnp.int32, (topk, 1), 0)]
    for a in range(1, SUBLANES):
        pieces.append(jnp.where(io8 < (topk // (a + 1)), v1[a:a + 1] + v2[0:SUBLANES], NEG_INF))
        flats.append(a * topk + io8)
    pieces.append(v1[SUBLANES:] + v2[0:1])
    flats.append((SUBLANES + io8) * topk)
    return jnp.concatenate(pieces, axis=0), jnp.concatenate(flats, axis=0)


def _pair_counts(sums, rank, mtot, topk):
    t = sums.shape[1]
    sel = jnp.where(rank < topk, 1.0, 0.0)
    z = jnp.sum(sel * jnp.exp(jnp.where(rank < topk, sums, mtot) - mtot), axis=0, keepdims=True)
    ro = lax.broadcasted_iota(jnp.int32, (topk, t), 0)
    nb = jnp.zeros((topk, t), F32)
    nb = jnp.where(ro == 0, jnp.sum(sel[0:topk], axis=0, keepdims=True), nb)
    for a in range(1, SUBLANES):
        lo = topk + (a - 1) * SUBLANES
        nb = jnp.where(ro == a, jnp.sum(sel[lo:lo + SUBLANES], axis=0, keepdims=True), nb)
    tail = sel[topk + (SUBLANES - 1) * SUBLANES:]
    nb = jnp.where(ro >= SUBLANES, jnp.concatenate([tail, tail], axis=0), nb)
    return nb, z


def _peer_select_kernel(h_ref, g_ref, wqt_ref, keys_ref, xn_ref, r2_ref, e2_ref, cnt_ref, c1_ref, q_scr,
                        *, heads, half, topk, group):
    xf = _rmsnorm(h_ref[...], g_ref[...])
    xn = xf.astype(BF16)
    xn_ref[...] = pltpu.bitcast(xf.T.astype(BF16), jnp.uint32)
    q_scr[...] = lax.dot_general(wqt_ref[...], xn, (((1,), (1,)), ((), ())), preferred_element_type=F32)
    nkeys = keys_ref.shape[1]
    key_idx = lax.broadcasted_iota(jnp.int32, (nkeys, 1), 0)

    def select_heads(exact_ties):
        def per_group(gi, bad):
            hs = [gi * group + j for j in range(group)]
            scores = []
            for h in hs:
                base = pl.multiple_of(h * 2 * half, 2 * half)
                scores.append(_mm(keys_ref[h], q_scr[pl.ds(base, half), :]))
                scores.append(_mm(keys_ref[heads + h], q_scr[pl.ds(base + half, half), :]))
            with_rank = [exact_ties or k % 2 == 1 for k in range(2 * group)]
            vals, ranks, left = _extract_topk(scores, key_idx, topk, exact_ties, with_rank)
            cands = [_pair_candidates(vals[2 * j], vals[2 * j + 1], topk) for j in range(group)]
            _, pair_ranks, _ = _extract_topk([cd[0] for cd in cands], cands[0][1], topk, exact_ties)
            for j, h in enumerate(hs):
                v1, r1, v2, r2 = vals[2 * j], ranks[2 * j], vals[2 * j + 1], ranks[2 * j + 1]
                nb, z = _pair_counts(cands[j][0], pair_ranks[j], v1[0:1] + v2[0:1], topk)
                cnt = jnp.zeros(scores[2 * j].shape, F32)
                for a in range(topk):
                    is_a = (r1 == float(a)) if r1 is not None else (scores[2 * j] == v1[a:a + 1])
                    cnt = jnp.where(is_a, nb[a:a + 1], cnt)
                r2_ref[h, 0] = pltpu.bitcast(r2.astype(BF16), jnp.uint32)
                e2_ref[h, 0] = pltpu.bitcast(jnp.exp(scores[2 * j + 1] - v2[0:1]).astype(BF16), jnp.uint32)
                cnt_ref[h, 0] = cnt
                c1_ref[h, 0] = jnp.exp(scores[2 * j] - v1[0:1]) / z
            marks = [lf == NEG_INF for lf in left] + [rk < topk for rk in pair_ranks]
            return bad + _tie_count(marks, topk)

        return lax.fori_loop(0, heads // group, per_group, jnp.zeros((1, h_ref.shape[0]), F32))

    ties = select_heads(exact_ties=False)

    @pl.when(jnp.max(ties) > 0.0)
    def _():
        select_heads(exact_ties=True)


def _peer_select(h2d, g, wq_t, keys, *, heads, nkeys, half, topk, tms):
    n, d = h2d.shape
    qd = wq_t.shape[0]
    kern = functools.partial(_peer_select_kernel, heads=heads, half=half, topk=topk, group=4)
    assert tms == LANES
    sel_f32 = jax.ShapeDtypeStruct((heads, n // LANES, nkeys, LANES), F32)
    sel_pk = jax.ShapeDtypeStruct((heads, n // LANES, nkeys // 2, LANES), jnp.uint32)
    sel_spec = pl.BlockSpec((heads, 1, nkeys, LANES), lambda i: (0, i, 0, 0))
    pk_spec = pl.BlockSpec((heads, 1, nkeys // 2, LANES), lambda i: (0, i, 0, 0))
    return pl.pallas_call(
        kern,
        grid=(n // tms,),
        in_specs=[
            pl.BlockSpec((tms, d), lambda i: (i, 0)),
            pl.BlockSpec((1, d), lambda i: (0, 0)),
            pl.BlockSpec((qd, d), lambda i: (0, 0)),
            pl.BlockSpec(keys.shape, lambda i: (0, 0, 0)),
        ],
        out_specs=[pl.BlockSpec((d // 2, tms), lambda i: (0, i)), pk_spec, pk_spec, sel_spec, sel_spec],
        out_shape=[jax.ShapeDtypeStruct((d // 2, n), jnp.uint32), sel_pk, sel_pk, sel_f32, sel_f32],
        scratch_shapes=[pltpu.VMEM((qd, tms), F32)],
        compiler_params=_cparams(("parallel",)),
        name="peer_select",
    )(h2d, g, wq_t, keys)


def _pair_index(s, offset, n_pairs):
    return jnp.clip(s - offset, 0, n_pairs - 1)


def _peer_dense_kernel(xn_ref, u_ref, vt_ref, r2_ref, e2_ref, cnt_ref, c1_ref, h_ref, g_ref, y_ref,
                       acc_ref, ht0_ref, ht1_ref, w0_ref, w1_ref, *, heads, nkeys, eb, tm, n_eb, n_pairs):
    s = pl.program_id(0)
    eb_b = lax.rem(_pair_index(s, 1, n_pairs), n_eb)
    eb_c = lax.rem(_pair_index(s, 2, n_pairs), n_eb)

    @pl.when(s == 0)
    def _():
        ht1_ref[...] = jnp.zeros(ht1_ref.shape, F32)
        w0_ref[...] = jnp.zeros(w0_ref.shape, jnp.uint32)
        w1_ref[...] = jnp.zeros(w1_ref.shape, jnp.uint32)

    @pl.when(eb_c == 0)
    def _():
        acc_ref[...] = jnp.zeros(acc_ref.shape, F32)

    i0 = pl.multiple_of(eb_b * (eb // nkeys), SUBLANES)
    zero = jnp.zeros((nkeys, LANES), BF16)

    def step(ht_cur, ht_prev, w_cur, w_prev):
        d = acc_ref.shape[0]
        xn_t = pltpu.bitcast(xn_ref[...], BF16)
        for sub in range(eb // SUB_BLOCK):
            w_in = jnp.concatenate(
                [pltpu.bitcast(w_prev[lc, sub * (SUB_BLOCK // 2):(sub + 1) * (SUB_BLOCK // 2), :], BF16)
                 for lc in range(tm // LANES)], axis=1)
            for q in range(MM_PIECES):
                dr = slice(q * (d // MM_PIECES), (q + 1) * (d // MM_PIECES))
                dr2 = slice(q * (d // 2 // MM_PIECES), (q + 1) * (d // 2 // MM_PIECES))
                acc_ref[dr, :] += jnp.dot(pltpu.bitcast(vt_ref[sub, dr2, :], BF16), w_in,
                                          preferred_element_type=F32)
                ii = sub * SUBLANES + q
                rs = slice(ii * nkeys, (ii + 1) * nkeys)
                rs2 = slice(ii * (nkeys // 2), (ii + 1) * (nkeys // 2))
                for lc in range(tm // LANES):
                    gate = zero
                    for h in range(heads):
                        cnt = cnt_ref[h, lc, pl.ds(i0 + sub * SUBLANES, SUBLANES), :][q:q + 1]
                        c1 = c1_ref[h, lc, pl.ds(i0 + sub * SUBLANES, SUBLANES), :][q:q + 1]
                        cnt = jnp.broadcast_to(cnt, (nkeys, LANES)).astype(BF16)
                        c1 = jnp.broadcast_to(c1, (nkeys, LANES)).astype(BF16)
                        r2 = pltpu.bitcast(r2_ref[h, lc], BF16)
                        e2 = pltpu.bitcast(e2_ref[h, lc], BF16)
                        gate = gate + jnp.where(r2 < cnt, e2, zero) * c1
                    act = _gelu_exact(ht_prev[lc, rs, :]).astype(BF16)
                    w_cur[lc, rs2, :] = pltpu.bitcast(gate * act, jnp.uint32)
                res = jnp.dot(pltpu.bitcast(u_ref[rs2, :], BF16), xn_t, preferred_element_type=F32)
                for lc in range(tm // LANES):
                    ht_cur[lc, rs, :] = res[:, lc * LANES:(lc + 1) * LANES]

    parity = lax.rem(s, 2)

    @pl.when(parity == 0)
    def _():
        step(ht0_ref, ht1_ref, w0_ref, w1_ref)

    @pl.when(parity == 1)
    def _():
        step(ht1_ref, ht0_ref, w1_ref, w0_ref)

    @pl.when(jnp.logical_and(s >= 2, eb_c == n_eb - 1))
    def _():
        y_ref[...] = _rmsnorm(h_ref[...] + acc_ref[...].T, g_ref[...])


def _peer_dense(xn, u, vt, r2, e2, cnt, c1, h2d, g, *, heads, nkeys, eb, tm):
    n, d = h2d.shape
    n_exp = 2 * u.shape[0]
    assert eb % SUB_BLOCK == 0 and SUB_BLOCK == SUBLANES * nkeys == MM_PIECES * nkeys
    assert vt.shape == (n_exp // SUB_BLOCK, d // 2, SUB_BLOCK)
    n_eb = n_exp // eb
    n_pairs = (n // tm) * n_eb
    kern = functools.partial(_peer_dense_kernel, heads=heads, nkeys=nkeys, eb=eb, tm=tm, n_eb=n_eb, n_pairs=n_pairs)

    def tb(offset):
        return lambda s: _pair_index(s, offset, n_pairs) // n_eb

    def ebk(offset):
        return lambda s: lax.rem(_pair_index(s, offset, n_pairs), n_eb)

    sel_spec = pl.BlockSpec((heads, tm // LANES, nkeys, LANES), lambda s: (0, tb(1)(s), 0, 0))
    pk_spec = pl.BlockSpec((heads, tm // LANES, nkeys // 2, LANES), lambda s: (0, tb(1)(s), 0, 0))
    return pl.pallas_call(
        kern,
        grid=(n_pairs + 2,),
        in_specs=[
            pl.BlockSpec((d // 2, tm), lambda s: (0, tb(0)(s))),
            pl.BlockSpec((eb // 2, d), lambda s: (ebk(0)(s), 0)),
            pl.BlockSpec((eb // SUB_BLOCK, d // 2, SUB_BLOCK), lambda s: (ebk(2)(s), 0, 0)),
            pk_spec, pk_spec, sel_spec, sel_spec,
            pl.BlockSpec((tm, d), lambda s: (tb(2)(s), 0)),
            pl.BlockSpec((1, d), lambda s: (0, 0)),
        ],
        out_specs=pl.BlockSpec((tm, d), lambda s: (tb(2)(s), 0)),
        out_shape=jax.ShapeDtypeStruct((n, d), F32),
        scratch_shapes=[
            pltpu.VMEM((d, tm), F32),
            pltpu.VMEM((tm // LANES, eb, LANES), F32),
            pltpu.VMEM((tm // LANES, eb, LANES), F32),
            pltpu.VMEM((tm // LANES, eb // 2, LANES), jnp.uint32),
            pltpu.VMEM((tm // LANES, eb // 2, LANES), jnp.uint32),
        ],
        compiler_params=_cparams(("arbitrary",)),
        name="peer_dense",
    )(xn, u, vt, r2, e2, cnt, c1, h2d, g)


def _pick_block(n, candidates):
    for c in candidates:
        if n % c == 0:
            return c
    raise ValueError(f"no block size in {candidates} divides {n}")


def _trunk(x, buf_a, buf_q, s0, wts, dims):
    heads, hd, cw, p_heads, nkeys, half = dims
    (g_mix, w_in, w_conv_a, w_conv_q, a_log_pad, dt_pad, g_dn, w_out_a, w_out_dn, w_o, g_ffn,
     wq_t, keys, u_pk, vt_pk, g_final) = wts
    bsz, tp, d = x.shape
    n = bsz * tp
    x2d = x.reshape(n, d)
    c = DN_CHUNK if tp % DN_CHUNK == 0 else SUBLANES
    ka, kq = w_conv_a.shape[0], w_conv_q.shape[0]
    last = tp - (pl.cdiv(tp, c) - 1) * c
    assert (tp % c == 0 or tp < c) and last >= kq - 1 and last >= ka - 1, (tp, c)

    proj = _in_proj(x2d, g_mix, w_in, _pick_block(n, (256, 128)))
    if c == DN_CHUNK:
        bb, unroll = _pick_block(bsz, (4, 2, 1)), 2
    else:
        bb, unroll = _pick_block(bsz, (16, 8, 4, 2, 1)), 16
    ya, og, nbuf_a, nbuf_q, s_new = _mixer(
        proj.reshape(bsz, tp, -1), buf_a, buf_q, s0, w_conv_a, w_conv_q, a_log_pad, dt_pad, g_dn,
        bb=bb, c=c, t_real=tp, heads=heads, hd=hd, unroll=min(unroll, bb))
    tm2 = _pick_block(n, (512, 256, 128))
    hres = _out_proj(x2d, ya.reshape(n, cw), og.reshape(n, cw), proj, w_out_a, w_out_dn, w_o, tm2,
                     gate_block=(6 * cw) // d)
    xn_t, r2, e2, cnt, c1 = _peer_select(hres, g_ffn, wq_t, keys, heads=p_heads, nkeys=nkeys, half=half,
                                         topk=PEER_TOPK, tms=LANES)
    tm3 = _pick_block(n, (512, 256, 128))
    y = _peer_dense(xn_t, u_pk, vt_pk, r2, e2, cnt, c1, hres, g_final, heads=p_heads, nkeys=nkeys,
                    eb=EXPERT_BLOCK, tm=tm3)
    return y.reshape(bsz, tp, d), nbuf_a, nbuf_q, s_new


def kernel(x_prompt, x_sample, state_conv_a, state_conv_qkv, state_delta, g_norm_mix, w_in, w_conv_a, w_conv_qkv,
           a_log, dt_bias, g_dn_norm, w_out_a, w_out_dn, w_o, g_norm_ffn, w_query, sub_keys, expert_u, expert_v,
           g_norm_final):
    depth = w_in.shape[0]
    assert depth == 1
    d = x_prompt.shape[-1]
    heads = a_log.shape[1]
    hd = g_dn_norm.shape[1]
    cw = heads * hd
    assert w_conv_a.shape[2] == cw and w_conv_qkv.shape[2] == 3 * cw and d == 2 * cw
    p_heads, nkeys, half = sub_keys.shape[2], sub_keys.shape[3], sub_keys.shape[4]
    assert nkeys == LANES and expert_u.shape[1] == nkeys * nkeys

    w = w_in[0]
    o_qkv, o_z, o_a, o_b, o_g = 3 * cw, 6 * cw, 7 * cw, 7 * cw + heads, 7 * cw + 2 * heads
    ab_pad = jnp.zeros((d, LANES - 2 * heads), w.dtype)
    w_r = jnp.concatenate([w[:, :o_z], w[:, o_g:], w[:, o_z:o_a], w[:, o_a:o_g], ab_pad], axis=1).astype(BF16)
    pad4 = lambda v: jnp.pad(v.astype(F32), ((0, 0), (0, LANES - v.shape[1])))
    wts = (
        g_norm_mix, w_r, w_conv_a[0], w_conv_qkv[0], pad4(a_log), pad4(dt_bias), g_dn_norm,
        w_out_a[0].astype(BF16), w_out_dn[0].astype(BF16), w_o[0].astype(BF16), g_norm_ffn,
        w_query[0].T.astype(BF16), sub_keys[0].reshape(2 * p_heads, nkeys, half).astype(BF16),
        *_pack_experts(expert_u[0], expert_v[0], SUB_BLOCK), g_norm_final.reshape(1, d),
    )
    dims = (heads, hd, cw, p_heads, nkeys, half)

    bp, tpr, _ = x_prompt.shape
    assert tpr % DN_CHUNK == 0
    ka, kq = w_conv_a.shape[1], w_conv_qkv.shape[1]
    zero_a = jnp.zeros((bp, ka - 1, cw), F32)
    zero_q = jnp.zeros((bp, kq - 1, 3 * cw), F32)
    zero_s = jnp.zeros((bp, heads, hd, hd), F32)
    y_p, a_p, q_p, s_p = _trunk(x_prompt, zero_a, zero_q, zero_s, wts, dims)

    assert x_sample.shape[1] <= SUBLANES
    y_s, a_s, q_s, s_s = _trunk(x_sample, state_conv_a[0], state_conv_qkv[0], state_delta[0], wts, dims)
    return (y_p, y_s, a_p[None], q_p[None], s_p[None], a_s[None], q_s[None], s_s[None])
```

```python
import functools

import jax
import jax.numpy as jnp
from jax import lax
from jax.experimental import pallas as pl
from jax.experimental.pallas import tpu as pltpu

EPS = 1e-6
F32 = jnp.float32
BF16 = jnp.bfloat16
LANES = 128
SUBLANES = 8
VMEM_LIMIT = 56 * 1024 * 1024

PEER_TOPK = 16
DN_CHUNK = 128
MM_PIECES = 8
SUB_BLOCK = 1024
EXPERT_BLOCK = 2048
NEG_INF = float("-inf")


def _cparams(sem):
    return pltpu.CompilerParams(dimension_semantics=sem, vmem_limit_bytes=VMEM_LIMIT)


def _mm(a, b):
    return jnp.dot(a.astype(BF16), b.astype(BF16), preferred_element_type=F32)


def _mm_nt(a, b):
    return lax.dot_general(a.astype(BF16), b.astype(BF16), (((1,), (1,)), ((), ())), preferred_element_type=F32)


def _mm_tn(a, b):
    return lax.dot_general(a.astype(BF16), b.astype(BF16), (((0,), (0,)), ((), ())), preferred_element_type=F32)


def _rmsnorm(x, g):
    return x * lax.rsqrt(jnp.mean(x * x, axis=-1, keepdims=True) + EPS) * g


def _sigmoid(x):
    return 1.0 / (1.0 + jnp.exp(-x))


def _silu(x):
    return x * _sigmoid(x)


def _softplus(x):
    return jnp.maximum(x, 0.0) + jnp.log(1.0 + jnp.exp(-jnp.abs(x)))


def _gelu_exact(x):
    return 0.5 * x * (1.0 + lax.erf(x * (2.0 ** -0.5)))


def _in_proj_kernel(x_ref, g_ref, w_ref, o_ref):
    xn = _rmsnorm(x_ref[...], g_ref[...]).astype(BF16)
    o_ref[...] = jnp.dot(xn, w_ref[...], preferred_element_type=F32)


def _in_proj(x2d, g, w, tm):
    n, d = x2d.shape
    width = w.shape[1]
    return pl.pallas_call(
        _in_proj_kernel,
        grid=(n // tm,),
        in_specs=[
            pl.BlockSpec((tm, d), lambda i: (i, 0)),
            pl.BlockSpec((1, d), lambda i: (0, 0)),
            pl.BlockSpec((d, width), lambda i: (0, 0)),
        ],
        out_specs=pl.BlockSpec((tm, width), lambda i: (i, 0)),
        out_shape=jax.ShapeDtypeStruct((n, width), F32),
        compiler_params=_cparams(("parallel",)),
        name="in_proj",
    )(x2d, g, w)


def _shift_rows(xp, k, rows):
    if k == 0:
        return xp[SUBLANES:SUBLANES + rows]
    return pltpu.roll(xp, k, axis=0)[SUBLANES:SUBLANES + rows]


def _cumsum_rows(x, rows):
    ridx = lax.broadcasted_iota(jnp.int32, x.shape, 0)
    s = 1
    while s < rows:
        x = x + jnp.where(ridx >= s, pltpu.roll(x, s, axis=0), 0.0)
        s *= 2
    return x


def _unit_lower_inverse_many(ms, c):
    eye = (lax.broadcasted_iota(jnp.int32, (c, c), 0) == lax.broadcasted_iota(jnp.int32, (c, c), 1)).astype(F32)
    n_factors = c.bit_length() - 1
    assert 1 << n_factors == c
    ps = [-m for m in ms]
    ts = [eye + p for p in ps]
    if n_factors == 1:
        return ts
    ps = [_mm(p, p) for p in ps]
    for _ in range(n_factors - 2):
        tps = [_mm(jnp.concatenate([t, p], axis=0), p) for t, p in zip(ts, ps)]
        ts = [t + tp[:c] for t, tp in zip(ts, tps)]
        ps = [tp[c:] for tp in tps]
    return [t + _mm(t, p) for t, p in zip(ts, ps)]


def _mixer_kernel(bch_ref, qkv_ref, z_ref, ab_ref, bufa_ref, bufq_ref, s0_ref,
                  wca_ref, wcq_ref, alog_ref, dtb_ref, gdn_ref,
                  ya_ref, og_ref, nbufa_ref, nbufq_ref, sout_ref,
                  halo_a, halo_q, state,
                  *, bb, c, heads, hd, t_real, n_chunks, unroll):
    ci = pl.program_id(1)
    cw = heads * hd
    ka = wca_ref.shape[0]
    kq = wcq_ref.shape[0]

    @pl.when(ci == 0)
    def _():
        state[...] = s0_ref[...]
        halo_a[...] = jnp.zeros(halo_a.shape, F32)
        halo_q[...] = jnp.zeros(halo_q.shape, F32)
        halo_a[:, SUBLANES - (ka - 1):, :] = bufa_ref[...]
        halo_q[:, SUBLANES - (kq - 1):, :] = bufq_ref[...]

    wca = wca_ref[...]
    wcq = wcq_ref[...]
    neg_a = -jnp.exp(alog_ref[...])
    dtb = dtb_ref[...]
    gdn = gdn_ref[...]
    rows_left = t_real - ci * c
    ridx = lax.broadcasted_iota(jnp.int32, (c, 1), 0)
    valid = ridx < rows_left
    ri = lax.broadcasted_iota(jnp.int32, (c, c), 0)
    si = lax.broadcasted_iota(jnp.int32, (c, c), 1)
    causal = ri >= si
    strict = ri > si
    cpad = max(c, LANES)

    rb = bch_ref.shape[1]

    def pad_rows(x):
        if rb == c:
            return x
        return jnp.concatenate([x, jnp.zeros((c - rb, x.shape[1]), x.dtype)], axis=0)

    def load_seq(b):
        return (pad_rows(bch_ref[b]), pad_rows(qkv_ref[b]), pad_rows(z_ref[b]), pad_rows(ab_ref[b]),
                halo_a[b], halo_q[b], [state[b, h] for h in range(heads)])

    def prep_seq(bch, qkv, z, ab, ha, hq, s_prev_all):
        b_c, c_c, h_c = bch[:, :cw], bch[:, cw:2 * cw], bch[:, 2 * cw:]
        ta = c_c * h_c
        xpa = jnp.concatenate([ha, ta], axis=0)
        conv_a = wca[ka - 1:ka] * _shift_rows(xpa, 0, c)
        for j in range(1, ka):
            conv_a = conv_a + wca[ka - 1 - j:ka - j] * _shift_rows(xpa, j, c)
        ya = b_c * conv_a
        xpq = jnp.concatenate([hq, qkv], axis=0)
        conv_q = wcq[kq - 1:kq] * _shift_rows(xpq, 0, c)
        for j in range(1, kq):
            conv_q = conv_q + wcq[kq - 1 - j:kq - j] * _shift_rows(xpq, j, c)
        conv_q = _silu(conv_q)
        new_ha, new_hq = xpa[c:c + SUBLANES], xpq[c:c + SUBLANES]
        g_all = jnp.where(valid, neg_a * _softplus(ab + dtb), 0.0)
        beta_all = jnp.where(valid, _sigmoid(ab), 0.0)
        gcum = _cumsum_rows(g_all, c)
        if c < cpad:
            gpad = jnp.concatenate([gcum, jnp.zeros((cpad - c, LANES), F32)], axis=0)
        else:
            gpad = gcum
        gcum_t = gpad.T
        chains = []
        for h in range(heads):
            q = conv_q[:, h * hd:(h + 1) * hd]
            k = conv_q[:, cw + h * hd:cw + (h + 1) * hd]
            v = conv_q[:, 2 * cw + h * hd:2 * cw + (h + 1) * hd]
            q = q * lax.rsqrt(jnp.sum(q * q, axis=-1, keepdims=True) + EPS) * (hd ** -0.5)
            k = k * lax.rsqrt(jnp.sum(k * k, axis=-1, keepdims=True) + EPS)
            q = jnp.where(valid, q, 0.0)
            k = jnp.where(valid, k, 0.0)
            v = jnp.where(valid, v, 0.0)
            gc = gcum[:, h:h + 1]
            gr = gcum_t[h:h + 1, :c]
            beta = beta_all[:, heads + h:heads + h + 1]
            g_last = gcum[c - 1:c, h:h + 1]
            eg = jnp.exp(gc)
            chains.append(dict(
                q=q, k=k, beta=beta, g_last=g_last, s_prev=s_prev_all[h], z=z[:, h * hd:(h + 1) * hd],
                decay=jnp.exp(jnp.where(causal, gc - gr, NEG_INF)),
                rhs=jnp.concatenate([v * beta, k * (beta * eg)], axis=-1),
                q_dec=q * eg, k_dec=k * jnp.exp(g_last - gc)))
        return ya, new_ha, new_hq, chains

    def delta_chains(ch):
        kkqk = [_mm_nt(jnp.concatenate([x["k"], x["q"]], axis=0), x["k"]) for x in ch]
        ms = [jnp.where(strict, kq_[:c] * x["decay"] * x["beta"], 0.0) for kq_, x in zip(kkqk, ch)]
        qks = [kq_[c:] * x["decay"] for kq_, x in zip(kkqk, ch)]
        t_invs = _unit_lower_inverse_many(ms, c)
        sols = [_mm(t, x["rhs"]) for t, x in zip(t_invs, ch)]
        ws_qs = [_mm(jnp.concatenate([sol[:, hd:], x["q_dec"]], axis=0), x["s_prev"])
                 for sol, x in zip(sols, ch)]
        v_news = [sol[:, :hd] - wq[:c] for sol, wq in zip(sols, ws_qs)]
        if c >= LANES:
            rs = [_mm(jnp.concatenate([qk, x["k_dec"].T], axis=0), vn)
                  for qk, x, vn in zip(qks, ch, v_news)]
            os_ = [wq[c:] + r[:c] for wq, r in zip(ws_qs, rs)]
            upd = [r[c:] for r in rs]
        else:
            os_ = [wq[c:] + _mm(qk, vn) for wq, qk, vn in zip(ws_qs, qks, v_news)]
            upd = [_mm_tn(x["k_dec"], vn) for x, vn in zip(ch, v_news)]
        states = [x["s_prev"] * jnp.exp(x["g_last"]) + u_ for x, u_ in zip(ch, upd)]
        outs = [o * lax.rsqrt(jnp.mean(o * o, axis=-1, keepdims=True) + EPS) * gdn * _silu(x["z"])
                for o, x in zip(os_, ch)]
        return outs, states

    def store_seq(b, ya, og, new_ha, new_hq, new_states):
        ya_ref[b] = ya[:rb]
        og_ref[b] = og[:rb]
        halo_a[b] = new_ha
        halo_q[b] = new_hq
        for h in range(heads):
            state[b, h] = new_states[h]

    def per_group(gi, carry):
        b0 = gi * unroll
        preps = [prep_seq(*load_seq(b0 + j)) for j in range(unroll)]
        outs, states = delta_chains([x for pr in preps for x in pr[3]])
        for j, (ya, new_ha, new_hq, _) in enumerate(preps):
            og = jnp.concatenate(outs[j * heads:(j + 1) * heads], axis=-1)
            store_seq(b0 + j, ya, og, new_ha, new_hq, states[j * heads:(j + 1) * heads])
        return carry

    lax.fori_loop(0, bb // unroll, per_group, 0)

    @pl.when(ci == n_chunks - 1)
    def _():
        sout_ref[...] = state[...]
        last = t_real - (n_chunks - 1) * c
        nbufa_ref[...] = (bch_ref[:, last - (ka - 1):last, cw:2 * cw]
                          * bch_ref[:, last - (ka - 1):last, 2 * cw:])
        nbufq_ref[...] = qkv_ref[:, last - (kq - 1):last, :]


def _mixer(proj3, buf_a, buf_q, s0, w_conv_a, w_conv_q, a_log_pad, dt_pad, g_dn, *, bb, c, t_real, heads, hd,
           unroll):
    bsz, tp, _ = proj3.shape
    cw = heads * hd
    n_chunks = pl.cdiv(tp, c)
    rb = min(c, tp)
    assert n_chunks * rb == tp
    ka, kq = w_conv_a.shape[0], w_conv_q.shape[0]
    kern = functools.partial(_mixer_kernel, bb=bb, c=c, heads=heads, hd=hd, t_real=t_real, n_chunks=n_chunks,
                             unroll=unroll)
    return pl.pallas_call(
        kern,
        grid=(bsz // bb, n_chunks),
        in_specs=[
            pl.BlockSpec((bb, rb, 3 * cw), lambda i, j: (i, j, 0)),
            pl.BlockSpec((bb, rb, 3 * cw), lambda i, j: (i, j, 1)),
            pl.BlockSpec((bb, rb, cw), lambda i, j: (i, j, (3 * cw + 3 * cw + 4 * cw) // cw)),
            pl.BlockSpec((bb, rb, LANES), lambda i, j: (i, j, (3 * cw + 3 * cw + 4 * cw + cw) // LANES)),
            pl.BlockSpec((bb, ka - 1, cw), lambda i, j: (i, 0, 0)),
            pl.BlockSpec((bb, kq - 1, 3 * cw), lambda i, j: (i, 0, 0)),
            pl.BlockSpec((bb, heads, hd, hd), lambda i, j: (i, 0, 0, 0)),
            pl.BlockSpec((ka, cw), lambda i, j: (0, 0)),
            pl.BlockSpec((kq, 3 * cw), lambda i, j: (0, 0)),
            pl.BlockSpec((1, LANES), lambda i, j: (0, 0)),
            pl.BlockSpec((1, LANES), lambda i, j: (0, 0)),
            pl.BlockSpec((1, hd), lambda i, j: (0, 0)),
        ],
        out_specs=[
            pl.BlockSpec((bb, rb, cw), lambda i, j: (i, j, 0)),
            pl.BlockSpec((bb, rb, cw), lambda i, j: (i, j, 0)),
            pl.BlockSpec((bb, ka - 1, cw), lambda i, j: (i, 0, 0)),
            pl.BlockSpec((bb, kq - 1, 3 * cw), lambda i, j: (i, 0, 0)),
            pl.BlockSpec((bb, heads, hd, hd), lambda i, j: (i, 0, 0, 0)),
        ],
        out_shape=[
            jax.ShapeDtypeStruct((bsz, tp, cw), F32),
            jax.ShapeDtypeStruct((bsz, tp, cw), F32),
            jax.ShapeDtypeStruct((bsz, ka - 1, cw), F32),
            jax.ShapeDtypeStruct((bsz, kq - 1, 3 * cw), F32),
            jax.ShapeDtypeStruct((bsz, heads, hd, hd), F32),
        ],
        scratch_shapes=[
            pltpu.VMEM((bb, SUBLANES, cw), F32),
            pltpu.VMEM((bb, SUBLANES, 3 * cw), F32),
            pltpu.VMEM((bb, heads, hd, hd), F32),
        ],
        compiler_params=_cparams(("parallel", "arbitrary")),
        name="mixer",
    )(proj3, proj3, proj3, proj3, buf_a, buf_q, s0, w_conv_a, w_conv_q, a_log_pad, dt_pad, g_dn)


def _out_proj_kernel(x_ref, ya_ref, og_ref, ga_ref, gb_ref, woa_ref, wod_ref, wo_ref, h_ref):
    y_a = jnp.dot(ya_ref[...].astype(BF16), woa_ref[...], preferred_element_type=F32)
    y_b = jnp.dot(og_ref[...].astype(BF16), wod_ref[...], preferred_element_type=F32)
    m = _sigmoid(ga_ref[...]) * y_a + _sigmoid(gb_ref[...]) * y_b
    h_ref[...] = x_ref[...] + jnp.dot(m.astype(BF16), wo_ref[...], preferred_element_type=F32)


def _out_proj(x2d, ya, og, proj, w_out_a, w_out_dn, w_o, tm, gate_block):
    n, d = x2d.shape
    cw = ya.shape[1]
    return pl.pallas_call(
        _out_proj_kernel,
        grid=(n // tm,),
        in_specs=[
            pl.BlockSpec((tm, d), lambda i: (i, 0)),
            pl.BlockSpec((tm, cw), lambda i: (i, 0)),
            pl.BlockSpec((tm, cw), lambda i: (i, 0)),
            pl.BlockSpec((tm, d), lambda i: (i, gate_block)),
            pl.BlockSpec((tm, d), lambda i: (i, gate_block + 1)),
            pl.BlockSpec((cw, d), lambda i: (0, 0)),
            pl.BlockSpec((cw, d), lambda i: (0, 0)),
            pl.BlockSpec((d, d), lambda i: (0, 0)),
        ],
        out_specs=pl.BlockSpec((tm, d), lambda i: (i, 0)),
        out_shape=jax.ShapeDtypeStruct((n, d), F32),
        compiler_params=_cparams(("parallel",)),
        name="out_proj",
    )(x2d, ya, og, proj, proj, w_out_a, w_out_dn, w_o)


def _pack_u_kernel(u_ref, o_ref):
    o_ref[...] = pltpu.bitcast(u_ref[...].astype(BF16), jnp.uint32)


def _pack_vt_kernel(v_ref, o_ref):
    o_ref[0] = pltpu.bitcast(v_ref[...].T.astype(BF16), jnp.uint32)


def _pack_experts(expert_u, expert_v, rows):
    n_exp, d = expert_u.shape
    u_pk = pl.pallas_call(
        _pack_u_kernel,
        grid=(n_exp // rows,),
        in_specs=[pl.BlockSpec((rows, d), lambda i: (i, 0))],
        out_specs=pl.BlockSpec((rows // 2, d), lambda i: (i, 0)),
        out_shape=jax.ShapeDtypeStruct((n_exp // 2, d), jnp.uint32),
        compiler_params=_cparams(("parallel",)),
        name="pack_u",
    )(expert_u)
    vt_pk = pl.pallas_call(
        _pack_vt_kernel,
        grid=(n_exp // rows,),
        in_specs=[pl.BlockSpec((rows, d), lambda i: (i, 0))],
        out_specs=pl.BlockSpec((1, d // 2, rows), lambda i: (i, 0, 0)),
        out_shape=jax.ShapeDtypeStruct((n_exp // rows, d // 2, rows), jnp.uint32),
        compiler_params=_cparams(("parallel",)),
        name="pack_vt",
    )(expert_v)
    return u_pk, vt_pk


def _extract_topk(ss, flat, topk, exact_ties, with_rank=None):
    n, t = ss[0].shape
    big = jnp.int32(2 ** 30)
    ro = lax.broadcasted_iota(jnp.int32, (topk, t), 0)
    with_rank = with_rank or [True] * len(ss)
    cur = list(ss)
    ranks = [jnp.full((n, t), float(topk), F32) if wr else None for wr in with_rank]
    vals = [jnp.zeros((topk, t), F32) for _ in ss]
    for r in range(topk):
        for i in range(len(ss)):
            m = jnp.max(cur[i], axis=0, keepdims=True)
            hit = cur[i] == m
            if exact_ties:
                hit = flat == jnp.min(jnp.where(hit, flat, big), axis=0, keepdims=True)
            if with_rank[i]:
                ranks[i] = jnp.where(hit, float(r), ranks[i])
            cur[i] = jnp.where(hit, NEG_INF, cur[i])
            vals[i] = jnp.where(ro == r, m, vals[i])
    return vals, ranks, cur


def _tie_count(marks, topk):
    bad = None
    for mk in marks:
        n_marked = jnp.sum(jnp.where(mk, 1.0, 0.0), axis=0, keepdims=True)
        bad = jnp.abs(n_marked - topk) if bad is None else bad + jnp.abs(n_marked - topk)
    return bad


def _pair_candidates(v1, v2, topk):
    io8 = lax.broadcasted_iota(jnp.int32, (SUBLANES, 1), 0)
    pieces = [v1[0:1] + v2]
    flats = [lax.broadcasted_iota(jnp.int32, (topk, 1), 0)]
    for a in range(1, SUBLANES):
        pieces.append(jnp.where(io8 < (topk // (a + 1)), v1[a:a + 1] + v2[0:SUBLANES], NEG_INF))
        flats.append(a * topk + io8)
    pieces.append(v1[SUBLANES:] + v2[0:1])
    flats.append((SUBLANES + io8) * topk)
    return jnp.concatenate(pieces, axis=0), jnp.concatenate(flats, axis=0)


def _pair_counts(sums, rank, mtot, topk):
    t = sums.shape[1]
    sel = jnp.where(rank < topk, 1.0, 0.0)
    z = jnp.sum(sel * jnp.exp(jnp.where(rank < topk, sums, mtot) - mtot), axis=0, keepdims=True)
    ro = lax.broadcasted_iota(jnp.int32, (topk, t), 0)
    nb = jnp.zeros((topk, t), F32)
    nb = jnp.where(ro == 0, jnp.sum(sel[0:topk], axis=0, keepdims=True), nb)
    for a in range(1, SUBLANES):
        lo = topk + (a - 1) * SUBLANES
        nb = jnp.where(ro == a, jnp.sum(sel[lo:lo + SUBLANES], axis=0, keepdims=True), nb)
    tail = sel[topk + (SUBLANES - 1) * SUBLANES:]
    nb = jnp.where(ro >= SUBLANES, jnp.concatenate([tail, tail], axis=0), nb)
    return nb, z


def _peer_select_kernel(h_ref, g_ref, wqt_ref, keys_ref, xn_ref, r2_ref, e2_ref, cnt_ref, c1_ref, q_scr,
                        *, heads, half, topk, group):
    xf = _rmsnorm(h_ref[...], g_ref[...])
    xn = xf.astype(BF16)
    xn_ref[...] = pltpu.bitcast(xf.T.astype(BF16), jnp.uint32)
    q_scr[...] = lax.dot_general(wqt_ref[...], xn, (((1,), (1,)), ((), ())), preferred_element_type=F32)
    nkeys = keys_ref.shape[1]
    key_idx = lax.broadcasted_iota(jnp.int32, (nkeys, 1), 0)

    def select_heads(exact_ties):
        def per_group(gi, bad):
            hs = [gi * group + j for j in range(group)]
            scores = []
            for h in hs:
                base = pl.multiple_of(h * 2 * half, 2 * half)
                scores.append(_mm(keys_ref[h], q_scr[pl.ds(base, half), :]))
                scores.append(_mm(keys_ref[heads + h], q_scr[pl.ds(base + half, half), :]))
            with_rank = [exact_ties or k % 2 == 1 for k in range(2 * group)]
            vals, ranks, left = _extract_topk(scores, key_idx, topk, exact_ties, with_rank)
            cands = [_pair_candidates(vals[2 * j], vals[2 * j + 1], topk) for j in range(group)]
            _, pair_ranks, _ = _extract_topk([cd[0] for cd in cands], cands[0][1], topk, exact_ties)
            for j, h in enumerate(hs):
                v1, r1, v2, r2 = vals[2 * j], ranks[2 * j], vals[2 * j + 1], ranks[2 * j + 1]
                nb, z = _pair_counts(cands[j][0], pair_ranks[j], v1[0:1] + v2[0:1], topk)
                cnt = jnp.zeros(scores[2 * j].shape, F32)
                for a in range(topk):
                    is_a = (r1 == float(a)) if r1 is not None else (scores[2 * j] == v1[a:a + 1])
                    cnt = jnp.where(is_a, nb[a:a + 1], cnt)
                r2_ref[h, 0] = pltpu.bitcast(r2.astype(BF16), jnp.uint32)
                e2_ref[h, 0] = pltpu.bitcast(jnp.exp(scores[2 * j + 1] - v2[0:1]).astype(BF16), jnp.uint32)
                cnt_ref[h, 0] = cnt
                c1_ref[h, 0] = jnp.exp(scores[2 * j] - v1[0:1]) / z
            marks = [lf == NEG_INF for lf in left] + [rk < topk for rk in pair_ranks]
            return bad + _tie_count(marks, topk)

        return lax.fori_loop(0, heads // group, per_group, jnp.zeros((1, h_ref.shape[0]), F32))

    ties = select_heads(exact_ties=False)

    @pl.when(jnp.max(ties) > 0.0)
    def _():
        select_heads(exact_ties=True)


def _peer_select(h2d, g, wq_t, keys, *, heads, nkeys, half, topk, tms):
    n, d = h2d.shape
    qd = wq_t.shape[0]
    kern = functools.partial(_peer_select_kernel, heads=heads, half=half, topk=topk, group=8)
    assert tms == LANES
    sel_f32 = jax.ShapeDtypeStruct((heads, n // LANES, nkeys, LANES), F32)
    sel_pk = jax.ShapeDtypeStruct((heads, n // LANES, nkeys // 2, LANES), jnp.uint32)
    sel_spec = pl.BlockSpec((heads, 1, nkeys, LANES), lambda i: (0, i, 0, 0))
    pk_spec = pl.BlockSpec((heads, 1, nkeys // 2, LANES), lambda i: (0, i, 0, 0))
    return pl.pallas_call(
        kern,
        grid=(n // tms,),
        in_specs=[
            pl.BlockSpec((tms, d), lambda i: (i, 0)),
            pl.BlockSpec((1, d), lambda i: (0, 0)),
            pl.BlockSpec((qd, d), lambda i: (0, 0)),
            pl.BlockSpec(keys.shape, lambda i: (0, 0, 0)),
        ],
        out_specs=[pl.BlockSpec((d // 2, tms), lambda i: (0, i)), pk_spec, pk_spec, sel_spec, sel_spec],
        out_shape=[jax.ShapeDtypeStruct((d // 2, n), jnp.uint32), sel_pk, sel_pk, sel_f32, sel_f32],
        scratch_shapes=[pltpu.VMEM((qd, tms), F32)],
        compiler_params=_cparams(("parallel",)),
        name="peer_select",
    )(h2d, g, wq_t, keys)


def _pair_index(s, offset, n_pairs):
    return jnp.clip(s - offset, 0, n_pairs - 1)


def _peer_dense_kernel(xn_ref, u_ref, vt_ref, r2_ref, e2_ref, cnt_ref, c1_ref, h_ref, g_ref, y_ref,
                       acc_ref, ht0_ref, ht1_ref, w0_ref, w1_ref, *, heads, nkeys, eb, tm, n_eb, n_pairs):
    s = pl.program_id(0)
    eb_b = lax.rem(_pair_index(s, 1, n_pairs), n_eb)
    eb_c = lax.rem(_pair_index(s, 2, n_pairs), n_eb)

    @pl.when(s == 0)
    def _():
        ht1_ref[...] = jnp.zeros(ht1_ref.shape, F32)
        w0_ref[...] = jnp.zeros(w0_ref.shape, jnp.uint32)
        w1_ref[...] = jnp.zeros(w1_ref.shape, jnp.uint32)

    @pl.when(eb_c == 0)
    def _():
        acc_ref[...] = jnp.zeros(acc_ref.shape, F32)

    i0 = pl.multiple_of(eb_b * (eb // nkeys), SUBLANES)
    zero = jnp.zeros((nkeys, LANES), BF16)

    def step(ht_cur, ht_prev, w_cur, w_prev):
        d = acc_ref.shape[0]
        xn_t = pltpu.bitcast(xn_ref[...], BF16)
        for sub in range(eb // SUB_BLOCK):
            w_in = jnp.concatenate(
                [pltpu.bitcast(w_prev[lc, sub * (SUB_BLOCK // 2):(sub + 1) * (SUB_BLOCK // 2), :], BF16)
                 for lc in range(tm // LANES)], axis=1)
            for q in range(MM_PIECES):
                dr = slice(q * (d // MM_PIECES), (q + 1) * (d // MM_PIECES))
                dr2 = slice(q * (d // 2 // MM_PIECES), (q + 1) * (d // 2 // MM_PIECES))
                acc_ref[dr, :] += jnp.dot(pltpu.bitcast(vt_ref[sub, dr2, :], BF16), w_in,
                                          preferred_element_type=F32)
                ii = sub * SUBLANES + q
                rs = slice(ii * nkeys, (ii + 1) * nkeys)
                rs2 = slice(ii * (nkeys // 2), (ii + 1) * (nkeys // 2))
                for lc in range(tm // LANES):
                    gate = zero
                    for h in range(heads):
                        cnt = cnt_ref[h, lc, pl.ds(i0 + sub * SUBLANES, SUBLANES), :][q:q + 1]
                        c1 = c1_ref[h, lc, pl.ds(i0 + sub * SUBLANES, SUBLANES), :][q:q + 1]
                        cnt = jnp.broadcast_to(cnt, (nkeys, LANES)).astype(BF16)
                        c1 = jnp.broadcast_to(c1, (nkeys, LANES)).astype(BF16)
                        r2 = pltpu.bitcast(r2_ref[h, lc], BF16)
                        e2 = pltpu.bitcast(e2_ref[h, lc], BF16)
                        gate = gate + jnp.where(r2 < cnt, e2, zero) * c1
                    act = _gelu_exact(ht_prev[lc, rs, :]).astype(BF16)
                    w_cur[lc, rs2, :] = pltpu.bitcast(gate * act, jnp.uint32)
                res = jnp.dot(pltpu.bitcast(u_ref[rs2, :], BF16), xn_t, preferred_element_type=F32)
                for lc in range(tm // LANES):
                    ht_cur[lc, rs, :] = res[:, lc * LANES:(lc + 1) * LANES]

    parity = lax.rem(s, 2)

    @pl.when(parity == 0)
    def _():
        step(ht0_ref, ht1_ref, w0_ref, w1_ref)

    @pl.when(parity == 1)
    def _():
        step(ht1_ref, ht0_ref, w1_ref, w0_ref)

    @pl.when(jnp.logical_and(s >= 2, eb_c == n_eb - 1))
    def _():
        y_ref[...] = _rmsnorm(h_ref[...] + acc_ref[...].T, g_ref[...])


def _peer_dense(xn, u, vt, r2, e2, cnt, c1, h2d, g, *, heads, nkeys, eb, tm):
    n, d = h2d.shape
    n_exp = 2 * u.shape[0]
    assert eb % SUB_BLOCK == 0 and SUB_BLOCK == SUBLANES * nkeys == MM_PIECES * nkeys
    assert vt.shape == (n_exp // SUB_BLOCK, d // 2, SUB_BLOCK)
    n_eb = n_exp // eb
    n_pairs = (n // tm) * n_eb
    kern = functools.partial(_peer_dense_kernel, heads=heads, nkeys=nkeys, eb=eb, tm=tm, n_eb=n_eb, n_pairs=n_pairs)

    def tb(offset):
        return lambda s: _pair_index(s, offset, n_pairs) // n_eb

    def ebk(offset):
        return lambda s: lax.rem(_pair_index(s, offset, n_pairs), n_eb)

    sel_spec = pl.BlockSpec((heads, tm // LANES, nkeys, LANES), lambda s: (0, tb(1)(s), 0, 0))
    pk_spec = pl.BlockSpec((heads, tm // LANES, nkeys // 2, LANES), lambda s: (0, tb(1)(s), 0, 0))
    return pl.pallas_call(
        kern,
        grid=(n_pairs + 2,),
        in_specs=[
            pl.BlockSpec((d // 2, tm), lambda s: (0, tb(0)(s))),
            pl.BlockSpec((eb // 2, d), lambda s: (ebk(0)(s), 0)),
            pl.BlockSpec((eb // SUB_BLOCK, d // 2, SUB_BLOCK), lambda s: (ebk(2)(s), 0, 0)),
            pk_spec, pk_spec, sel_spec, sel_spec,
            pl.BlockSpec((tm, d), lambda s: (tb(2)(s), 0)),
            pl.BlockSpec((1, d), lambda s: (0, 0)),
        ],
        out_specs=pl.BlockSpec((tm, d), lambda s: (tb(2)(s), 0)),
        out_shape=jax.ShapeDtypeStruct((n, d), F32),
        scratch_shapes=[
            pltpu.VMEM((d, tm), F32),
            pltpu.VMEM((tm // LANES, eb, LANES), F32),
            pltpu.VMEM((tm // LANES, eb, LANES), F32),
            pltpu.VMEM((tm // LANES, eb // 2, LANES), jnp.uint32),
            pltpu.VMEM((tm // LANES, eb // 2, LANES), jnp.uint32),
        ],
        compiler_params=_cparams(("arbitrary",)),
        name="peer_dense",
    )(xn, u, vt, r2, e2, cnt, c1, h2d, g)


def _pick_block(n, candidates):
    for c in candidates:
        if n % c == 0:
            return c
    raise ValueError(f"no block size in {candidates} divides {n}")


def _trunk(x, buf_a, buf_q, s0, wts, dims):
    heads, hd, cw, p_heads, nkeys, half = dims
    (g_mix, w_in, w_conv_a, w_conv_q, a_log_pad, dt_pad, g_dn, w_out_a, w_out_dn, w_o, g_ffn,
     wq_t, keys, u_pk, vt_pk, g_final) = wts
    bsz, tp, d = x.shape
    n = bsz * tp
    x2d = x.reshape(n, d)
    c = DN_CHUNK if tp % DN_CHUNK == 0 else SUBLANES
    ka, kq = w_conv_a.shape[0], w_conv_q.shape[0]
    last = tp - (pl.cdiv(tp, c) - 1) * c
    assert (tp % c == 0 or tp < c) and last >= kq - 1 and last >= ka - 1, (tp, c)

    proj = _in_proj(x2d, g_mix, w_in, _pick_block(n, (256, 128)))
    if c == DN_CHUNK:
        bb, unroll = _pick_block(bsz, (4, 2, 1)), 2
    else:
        bb, unroll = _pick_block(bsz, (16, 8, 4, 2, 1)), 16
    ya, og, nbuf_a, nbuf_q, s_new = _mixer(
        proj.reshape(bsz, tp, -1), buf_a, buf_q, s0, w_conv_a, w_conv_q, a_log_pad, dt_pad, g_dn,
        bb=bb, c=c, t_real=tp, heads=heads, hd=hd, unroll=min(unroll, bb))
    tm2 = _pick_block(n, (512, 256, 128))
    hres = _out_proj(x2d, ya.reshape(n, cw), og.reshape(n, cw), proj, w_out_a, w_out_dn, w_o, tm2,
                     gate_block=(6 * cw) // d)
    xn_t, r2, e2, cnt, c1 = _peer_select(hres, g_ffn, wq_t, keys, heads=p_heads, nkeys=nkeys, half=half,
                                         topk=PEER_TOPK, tms=LANES)
    tm3 = _pick_block(n, (512, 256, 128))
    y = _peer_dense(xn_t, u_pk, vt_pk, r2, e2, cnt, c1, hres, g_final, heads=p_heads, nkeys=nkeys,
                    eb=EXPERT_BLOCK, tm=tm3)
    return y.reshape(bsz, tp, d), nbuf_a, nbuf_q, s_new


def kernel(x_prompt, x_sample, state_conv_a, state_conv_qkv, state_delta, g_norm_mix, w_in, w_conv_a, w_conv_qkv,
           a_log, dt_bias, g_dn_norm, w_out_a, w_out_dn, w_o, g_norm_ffn, w_query, sub_keys, expert_u, expert_v,
           g_norm_final):
    depth = w_in.shape[0]
    assert depth == 1
    d = x_prompt.shape[-1]
    heads = a_log.shape[1]
    hd = g_dn_norm.shape[1]
    cw = heads * hd
    assert w_conv_a.shape[2] == cw and w_conv_qkv.shape[2] == 3 * cw and d == 2 * cw
    p_heads, nkeys, half = sub_keys.shape[2], sub_keys.shape[3], sub_keys.shape[4]
    assert nkeys == LANES and expert_u.shape[1] == nkeys * nkeys

    w = w_in[0]
    o_qkv, o_z, o_a, o_b, o_g = 3 * cw, 6 * cw, 7 * cw, 7 * cw + heads, 7 * cw + 2 * heads
    ab_pad = jnp.zeros((d, LANES - 2 * heads), w.dtype)
    w_r = jnp.concatenate([w[:, :o_z], w[:, o_g:], w[:, o_z:o_a], w[:, o_a:o_g], ab_pad], axis=1).astype(BF16)
    pad4 = lambda v: jnp.pad(v.astype(F32), ((0, 0), (0, LANES - v.shape[1])))
    wts = (
        g_norm_mix, w_r, w_conv_a[0], w_conv_qkv[0], pad4(a_log), pad4(dt_bias), g_dn_norm,
        w_out_a[0].astype(BF16), w_out_dn[0].astype(BF16), w_o[0].astype(BF16), g_norm_ffn,
        w_query[0].T.astype(BF16), sub_keys[0].reshape(2 * p_heads, nkeys, half).astype(BF16),
        *_pack_experts(expert_u[0], expert_v[0], SUB_BLOCK), g_norm_final.reshape(1, d),
    )
    dims = (heads, hd, cw, p_heads, nkeys, half)

    bp, tpr, _ = x_prompt.shape
    assert tpr % DN_CHUNK == 0
    ka, kq = w_conv_a.shape[1], w_conv_qkv.shape[1]
    zero_a = jnp.zeros((bp, ka - 1, cw), F32)
    zero_q = jnp.zeros((bp, kq - 1, 3 * cw), F32)
    zero_s = jnp.zeros((bp, heads, hd, hd), F32)
    y_p, a_p, q_p, s_p = _trunk(x_prompt, zero_a, zero_q, zero_s, wts, dims)

    assert x_sample.shape[1] <= SUBLANES
    y_s, a_s, q_s, s_s = _trunk(x_sample, state_conv_a[0], state_conv_qkv[0], state_delta[0], wts, dims)
    return (y_p, y_s, a_p[None], q_p[None], s_p[None], a_s[None], q_s[None], s_s[None])
```
